```python
import jax, jax.numpy as jnp
from jax import lax
import numpy as np

D_MODEL = 2048
BATCH = 2
SEQ = 16384
DEPTH = 2

N_MIXERS = 2
MIX_WIDTH = D_MODEL
TOK_WIDTH = MIX_WIDTH // 2
N_POOL_GROUPS = 4
POOL_GROUP_DIM = TOK_WIDTH // N_POOL_GROUPS
POOL_WINDOWS = (2, 4, 8, 16)
SGU_CHUNK = 128
N_SGU_GROUPS = 4
SGU_GROUP_DIM = TOK_WIDTH // N_SGU_GROUPS
N_MEM = 256
N_XA_HEADS = 4
XA_WIDTH = MIX_WIDTH - TOK_WIDTH
XA_HEAD_DIM = XA_WIDTH // N_XA_HEADS
N_GROUPS = 4
EXPERTS_PER_GROUP = 8
N_EXPERTS = N_GROUPS * EXPERTS_PER_GROUP
TOP_K_IN_GROUP = 2
D_EXPERT = D_MODEL // 2
MOE_BLOCK = 256
EPS = 1e-6

kernel_name = 'hybrid_pool_sgu_memxattn_hmoe'


def rms_norm(x, g):
    xf = x.astype(jnp.float32)
    y = xf * lax.rsqrt(jnp.mean(xf * xf, axis=-1, keepdims=True) + EPS)
    return (y * g.astype(jnp.float32)).astype(x.dtype)


def causal_window_mean(p, window):
    S = p.shape[1]
    cs = jnp.cumsum(p, axis=1)
    lag = jnp.pad(cs, ((0, 0), (window, 0), (0, 0)))[:, :S]
    count = jnp.minimum(jnp.arange(1, S + 1), window).astype(jnp.float32)
    return (cs - lag) / count[None, :, None]


def memory_cross_attention(q, mem_n, w_kv):
    B, S, _ = q.shape
    M = mem_n.shape[1]
    k, v = jnp.split(mem_n @ w_kv, 2, axis=-1)
    qh = q.reshape(B, S, N_XA_HEADS, XA_HEAD_DIM)
    kh = k.reshape(B, M, N_XA_HEADS, XA_HEAD_DIM)
    vh = v.reshape(B, M, N_XA_HEADS, XA_HEAD_DIM)
    s = jnp.einsum('bshd,bmhd->bhsm', qh, kh).astype(jnp.float32) * (XA_HEAD_DIM ** -0.5)
    p = jax.nn.softmax(s, axis=-1).astype(vh.dtype)
    return jnp.einsum('bhsm,bmhd->bshd', p, vh).reshape(B, S, XA_WIDTH)


def pool_layer(h, mem_n, w_in, w_grp, scale, w_kv, w_out):
    B, S, _ = h.shape
    proj = h @ w_in
    p, q = proj[..., :TOK_WIDTH], proj[..., TOK_WIDTH:]
    pf = p.astype(jnp.float32).reshape(B, S, N_POOL_GROUPS, POOL_GROUP_DIM)
    pooled = jnp.stack([causal_window_mean(pf[:, :, g], POOL_WINDOWS[g]) for g in range(N_POOL_GROUPS)], axis=2) - pf
    mixed = jnp.einsum('bsgc,gcd->bsgd', pooled.astype(h.dtype), w_grp).reshape(B, S, TOK_WIDTH) * scale
    attn = memory_cross_attention(q, mem_n, w_kv)
    return jnp.concatenate([mixed, attn], axis=-1) @ w_out


def sgu_layer(h, mem_n, w_in, norm_g, w_s, b_s, w_kv, w_out):
    B, S, _ = h.shape
    proj = h @ w_in
    u, v = jnp.split(jax.nn.gelu(proj[..., :2 * TOK_WIDTH]), 2, axis=-1)
    q = proj[..., 2 * TOK_WIDTH:]
    v = rms_norm(v, norm_g).reshape(B, S // SGU_CHUNK, SGU_CHUNK, N_SGU_GROUPS, SGU_GROUP_DIM)
    causal = jnp.tril(jnp.ones((SGU_CHUNK, SGU_CHUNK), dtype=bool))
    w_causal = jnp.where(causal[None], w_s, jnp.zeros_like(w_s))
    spatial = jnp.einsum('gts,bnsgc->bntgc', w_causal, v) + b_s.T[None, None, :, :, None]
    mixed = u * spatial.reshape(B, S, TOK_WIDTH)
    attn = memory_cross_attention(q, mem_n, w_kv)
    return jnp.concatenate([mixed, attn], axis=-1) @ w_out


def hierarchical_moe(h, w_group, b_group, w_expert, b_expert, w_gu, w_down):
    B, S, D = h.shape
    T = B * S
    ht = h.reshape(T, D)
    g_logits = (ht @ w_group).astype(jnp.float32) + b_group.astype(jnp.float32)
    g_prob = jax.nn.softmax(g_logits, axis=-1)
    g_idx = jnp.argmax(g_logits, axis=-1).astype(jnp.int32)
    p_g = jnp.take_along_axis(g_prob, g_idx[:, None], axis=1)[:, 0]
    e_logits = (ht @ w_expert).astype(jnp.float32) + b_expert.astype(jnp.float32)
    cols = g_idx[:, None] * EXPERTS_PER_GROUP + jnp.arange(EXPERTS_PER_GROUP, dtype=jnp.int32)[None]
    e_prob = jax.nn.softmax(jnp.take_along_axis(e_logits, cols, axis=1), axis=-1)
    top_p, top_i = lax.top_k(e_prob, TOP_K_IN_GROUP)
    top_p = top_p / jnp.sum(top_p, axis=-1, keepdims=True)
    gate = (p_g[:, None] * top_p).astype(h.dtype)
    expert_id = g_idx[:, None] * EXPERTS_PER_GROUP + top_i.astype(jnp.int32)
    A = T * TOP_K_IN_GROUP
    e_flat = expert_id.reshape(A)
    w_flat = gate.reshape(A)
    tok_flat = jnp.arange(A, dtype=jnp.int32) // TOP_K_IN_GROUP
    order = jnp.argsort(e_flat)
    sorted_e = e_flat[order]
    counts = jnp.zeros((N_EXPERTS,), jnp.int32).at[e_flat].add(1)
    starts = jnp.cumsum(counts) - counts
    padded = (counts + MOE_BLOCK - 1) // MOE_BLOCK * MOE_BLOCK
    pends = jnp.cumsum(padded)
    pstarts = pends - padded
    dest = pstarts[sorted_e] + (jnp.arange(A, dtype=jnp.int32) - starts[sorted_e])
    P = (A + N_EXPERTS * (MOE_BLOCK - 1) + MOE_BLOCK - 1) // MOE_BLOCK * MOE_BLOCK
    n_blocks = P // MOE_BLOCK
    buf_tok = jnp.full((P,), T, jnp.int32).at[dest].set(tok_flat[order])
    buf_w = jnp.zeros((P,), h.dtype).at[dest].set(w_flat[order])
    blk_e = jnp.minimum(jnp.searchsorted(pends, jnp.arange(n_blocks, dtype=jnp.int32) * MOE_BLOCK, side='right'), N_EXPERTS - 1).astype(jnp.int32)
    h_pad = jnp.concatenate([ht, jnp.zeros((1, D), ht.dtype)], axis=0)

    def expert_block(args):
        idx, e = args
        xb = h_pad[idx]
        g, u = jnp.split(xb @ w_gu[e], 2, axis=-1)
        return (jax.nn.silu(g) * u) @ w_down[e]

    out = lax.map(expert_block, (buf_tok.reshape(n_blocks, MOE_BLOCK), blk_e)).reshape(P, D)
    y = jnp.zeros((T + 1, D), h.dtype).at[buf_tok].add(out * buf_w[:, None])[:T]
    return y.reshape(B, S, D)


def setup_inputs(seed: int = 0) -> dict:
    key = jax.random.key(seed)
    ks = jax.random.split(key, 32)
    n_pool = (DEPTH + 1) // 2
    n_sgu = DEPTH // 2

    def nrm(k, shape, scale):
        return jax.random.normal(k, shape, jnp.float32) * scale

    return {
        'x': nrm(ks[0], (BATCH, SEQ, D_MODEL), 1.0),
        'mem': nrm(ks[1], (BATCH, N_MEM, D_MODEL), 1.0),
        'norm_mix_g': 1.0 + nrm(ks[2], (DEPTH, D_MODEL), 0.05),
        'norm_mem_g': 1.0 + nrm(ks[3], (DEPTH, D_MODEL), 0.05),
        'norm_ffn_g': 1.0 + nrm(ks[4], (DEPTH, D_MODEL), 0.05),
        'pool_w_in': nrm(ks[5], (n_pool, D_MODEL, TOK_WIDTH + XA_WIDTH), D_MODEL ** -0.5),
        'pool_w_grp': nrm(ks[6], (n_pool, N_POOL_GROUPS, POOL_GROUP_DIM, POOL_GROUP_DIM), POOL_GROUP_DIM ** -0.5),
        'pool_scale': 1.0 + nrm(ks[7], (n_pool, TOK_WIDTH), 0.1),
        'pool_w_out': nrm(ks[8], (n_pool, MIX_WIDTH, D_MODEL), MIX_WIDTH ** -0.5),
        'sgu_w_in': nrm(ks[9], (n_sgu, D_MODEL, 2 * TOK_WIDTH + XA_WIDTH), D_MODEL ** -0.5),
        'sgu_norm_g': 1.0 + nrm(ks[10], (n_sgu, TOK_WIDTH), 0.05),
        'sgu_w_s': nrm(ks[11], (n_sgu, N_SGU_GROUPS, SGU_CHUNK, SGU_CHUNK), SGU_CHUNK ** -0.5),
        'sgu_b_s': 1.0 + nrm(ks[12], (n_sgu, N_SGU_GROUPS, SGU_CHUNK), 0.1),
        'sgu_w_out': nrm(ks[13], (n_sgu, MIX_WIDTH, D_MODEL), MIX_WIDTH ** -0.5),
        'xa_w_kv': nrm(ks[14], (DEPTH, D_MODEL, 2 * XA_WIDTH), D_MODEL ** -0.5),
        'moe_w_group': nrm(ks[15], (DEPTH, D_MODEL, N_GROUPS), D_MODEL ** -0.5),
        'moe_b_group': nrm(ks[16], (DEPTH, N_GROUPS), 0.01),
        'moe_w_expert': nrm(ks[17], (DEPTH, D_MODEL, N_EXPERTS), D_MODEL ** -0.5),
        'moe_b_expert': nrm(ks[18], (DEPTH, N_EXPERTS), 0.01),
        'moe_w_gu': nrm(ks[19], (DEPTH, N_EXPERTS, D_MODEL, 2 * D_EXPERT), D_MODEL ** -0.5),
        'moe_w_down': nrm(ks[20], (DEPTH, N_EXPERTS, D_EXPERT, D_MODEL), D_EXPERT ** -0.5),
        'final_norm_g': 1.0 + nrm(ks[21], (D_MODEL,), 0.05),
    }


def reference(x, mem, norm_mix_g, norm_mem_g, norm_ffn_g, pool_w_in, pool_w_grp, pool_scale, pool_w_out,
              sgu_w_in, sgu_norm_g, sgu_w_s, sgu_b_s, sgu_w_out, xa_w_kv, moe_w_group, moe_b_group,
              moe_w_expert, moe_b_expert, moe_w_gu, moe_w_down, final_norm_g):
    for i in range(DEPTH):
        h = rms_norm(x, norm_mix_g[i])
        mem_n = rms_norm(mem, norm_mem_g[i])
        j = i // N_MIXERS
        if i % N_MIXERS == 0:
            x = x + pool_layer(h, mem_n, pool_w_in[j], pool_w_grp[j], pool_scale[j], xa_w_kv[i], pool_w_out[j])
        else:
            x = x + sgu_layer(h, mem_n, sgu_w_in[j], sgu_norm_g[j], sgu_w_s[j], sgu_b_s[j], xa_w_kv[i], sgu_w_out[j])
        h = rms_norm(x, norm_ffn_g[i])
        x = x + hierarchical_moe(h, moe_w_group[i], moe_b_group[i], moe_w_expert[i], moe_b_expert[i], moe_w_gu[i], moe_w_down[i])
    return rms_norm(x, final_norm_g)
```

```python
import functools

import jax
import jax.numpy as jnp
from jax import lax
from jax.experimental import pallas as pl
from jax.experimental.pallas import tpu as pltpu

D_MODEL = 2048
BATCH = 2
SEQ = 16384
DEPTH = 2
N_TOK = BATCH * SEQ
TOK_WIDTH = 1024
N_POOL_GROUPS = 4
POOL_GROUP_DIM = 256
POOL_WINDOWS = (2, 4, 8, 16)
POOL_HALO = 16
SGU_CHUNK = 128
N_SGU_GROUPS = 4
SGU_GROUP_DIM = 256
N_MEM = 256
N_XA_HEADS = 4
XA_WIDTH = 1024
XA_HEAD_DIM = 256
N_GROUPS = 4
EXPERTS_PER_GROUP = 8
N_EXPERTS = 32
TOP_K = 2
D_EXPERT = 1024
EPS = 1e-6

LANES = 128
TM = 512
BLOCKS_PER_SEQ = SEQ // TM
N_PAIRS = N_TOK * TOP_K
EXP_BLK = 256
N_SORTED = (N_PAIRS + N_EXPERTS * (EXP_BLK - 1) + EXP_BLK - 1) // EXP_BLK * EXP_BLK
N_EXP_BLOCKS = N_SORTED // EXP_BLK
PACKED = D_MODEL // 2
GROUP_LANE0 = N_EXPERTS
VMEM_LIMIT = 56 * 1024 * 1024

F32 = jnp.float32
BF16 = jnp.bfloat16


def _rms(x, g):
    ms = jnp.mean(x * x, axis=-1, keepdims=True)
    return x * lax.rsqrt(ms + EPS) * g


def _gelu_tanh(x):
    c = 0.7978845608028654
    return x * (0.5 * (1.0 + jnp.tanh(c * (x + 0.044715 * (x * x * x)))))


def _const_spec(shape):
    nd = len(shape)
    return pl.BlockSpec(shape, lambda *_: (0,) * nd, pipeline_mode=pl.Buffered(1))


def _kv_kernel(mem_ref, g_ref, w_ref, o_ref):
    j = pl.program_id(0)
    mn = _rms(mem_ref[...], g_ref[...]).astype(BF16)
    r = jnp.dot(mn, w_ref[...], preferred_element_type=F32)
    scale = jnp.where(j < 2, XA_HEAD_DIM ** -0.5, 1.0).astype(F32)
    o_ref[...] = (r * scale).astype(BF16)


def _kv_proj(mem2d, g, w_kv_bf16):
    nb = 4
    wn = 2 * XA_WIDTH // nb
    return pl.pallas_call(
        _kv_kernel,
        grid=(nb,),
        in_specs=[
            pl.BlockSpec((BATCH * N_MEM, D_MODEL), lambda j: (0, 0)),
            pl.BlockSpec((1, D_MODEL), lambda j: (0, 0)),
            pl.BlockSpec((D_MODEL, wn), lambda j: (0, j)),
        ],
        out_specs=pl.BlockSpec((BATCH * N_MEM, wn), lambda j: (0, j)),
        out_shape=jax.ShapeDtypeStruct((BATCH * N_MEM, 2 * XA_WIDTH), BF16),
        compiler_params=pltpu.CompilerParams(dimension_semantics=("arbitrary",), vmem_limit_bytes=VMEM_LIMIT),
        name="kv_proj",
    )(mem2d, g, w_kv_bf16)


def _cross_attention(h, w_in_ref, q_off, kv_ref, cat_ref):
    for hd in range(N_XA_HEADS):
        lo = hd * XA_HEAD_DIM
        q = jnp.dot(h, w_in_ref[:, q_off + lo:q_off + lo + XA_HEAD_DIM], preferred_element_type=F32).astype(BF16)
        k = kv_ref[:, lo:lo + XA_HEAD_DIM]
        v = kv_ref[:, XA_WIDTH + lo:XA_WIDTH + lo + XA_HEAD_DIM]
        s = lax.dot_general(q, k, (((1,), (1,)), ((), ())), preferred_element_type=F32)
        m = jnp.max(s, axis=-1, keepdims=True)
        e = jnp.exp(s - m)
        p = (e / jnp.sum(e, axis=-1, keepdims=True)).astype(BF16)
        o = jnp.dot(p, v, preferred_element_type=F32)
        cat_ref[:, TOK_WIDTH + lo:TOK_WIDTH + lo + XA_HEAD_DIM] = o.astype(BF16)


def _pool_kernel(x_ref, g_ref, w_in_ref, kv_ref, wgrp_ref, scale_ref, cat_ref, ext_ref):
    j = pl.program_id(0)
    jb = j % BLOCKS_PER_SEQ
    h = _rms(x_ref[...], g_ref[...]).astype(BF16)

    @pl.when(jb == 0)
    def _():
        ext_ref[0:POOL_HALO, :] = jnp.zeros((POOL_HALO, TOK_WIDTH), F32)

    pos = jb * TM + lax.broadcasted_iota(jnp.int32, (TM, 1), 0)
    for g in range(N_POOL_GROUPS):
        c0 = g * POOL_GROUP_DIM
        c1 = c0 + POOL_GROUP_DIM
        p = jnp.dot(h, w_in_ref[:, c0:c1], preferred_element_type=F32)
        ext_ref[POOL_HALO:POOL_HALO + TM, c0:c1] = p
        a = ext_ref[:, c0:c1]
        for k in range(g + 1):
            a = a + pltpu.roll(a, 1 << k, 0)
        cnt = jnp.minimum(pos + 1, POOL_WINDOWS[g]).astype(F32)
        pooled = a[POOL_HALO:, :] / cnt - p
        mixed = jnp.dot(pooled.astype(BF16), wgrp_ref[g], preferred_element_type=F32) * scale_ref[:, c0:c1]
        cat_ref[:, c0:c1] = mixed.astype(BF16)
    ext_ref[0:POOL_HALO, :] = ext_ref[TM:TM + POOL_HALO, :]
    _cross_attention(h, w_in_ref, TOK_WIDTH, kv_ref, cat_ref)


def _sgu_kernel(x_ref, g_ref, w_in_ref, kv_ref, ng_ref, ws_ref, bs_ref, cat_ref, v_ref):
    h = _rms(x_ref[...], g_ref[...]).astype(BF16)
    ss = jnp.zeros((TM, 1), F32)
    for g in range(N_SGU_GROUPS):
        c0 = g * SGU_GROUP_DIM
        c1 = c0 + SGU_GROUP_DIM
        gv = _gelu_tanh(jnp.dot(h, w_in_ref[:, TOK_WIDTH + c0:TOK_WIDTH + c1], preferred_element_type=F32))
        v_ref[:, c0:c1] = gv
        ss = ss + jnp.sum(gv * gv, axis=-1, keepdims=True)
    r = lax.rsqrt(ss * (1.0 / TOK_WIDTH) + EPS)
    row = lax.broadcasted_iota(jnp.int32, (SGU_CHUNK, SGU_CHUNK), 0)
    col = lax.broadcasted_iota(jnp.int32, (SGU_CHUNK, SGU_CHUNK), 1)
    causal = col <= row
    for g in range(N_SGU_GROUPS):
        c0 = g * SGU_GROUP_DIM
        c1 = c0 + SGU_GROUP_DIM
        u = _gelu_tanh(jnp.dot(h, w_in_ref[:, c0:c1], preferred_element_type=F32))
        vn = (v_ref[:, c0:c1] * r * ng_ref[:, c0:c1]).astype(BF16)
        wc = jnp.where(causal, ws_ref[g], 0.0).astype(BF16)
        for n in range(TM // SGU_CHUNK):
            r0 = n * SGU_CHUNK
            r1 = r0 + SGU_CHUNK
            sp = jnp.dot(wc, vn[r0:r1, :], preferred_element_type=F32) + bs_ref[:, c0:c1]
            cat_ref[r0:r1, c0:c1] = (u[r0:r1, :] * sp).astype(BF16)
    _cross_attention(h, w_in_ref, 2 * TOK_WIDTH, kv_ref, cat_ref)


def _mixer_call(kernel, n_in, x, g, w_in, kv, extra, extra_specs, scratch, name):
    return pl.pallas_call(
        kernel,
        grid=(N_TOK // TM,),
        in_specs=[
            pl.BlockSpec((TM, D_MODEL), lambda j: (j, 0)),
            _const_spec((1, D_MODEL)),
            _const_spec((D_MODEL, n_in)),
            pl.BlockSpec((N_MEM, 2 * XA_WIDTH), lambda j: (j // BLOCKS_PER_SEQ, 0)),
        ] + extra_specs,
        out_specs=pl.BlockSpec((TM, D_MODEL), lambda j: (j, 0)),
        out_shape=jax.ShapeDtypeStruct((N_TOK, D_MODEL), BF16),
        scratch_shapes=scratch,
        compiler_params=pltpu.CompilerParams(dimension_semantics=("arbitrary",), vmem_limit_bytes=VMEM_LIMIT),
        name=name,
    )(x, g, w_in, kv, *extra)


def _pool_mixer(x, g, w_in, kv, w_grp, scale):
    return _mixer_call(
        _pool_kernel, TOK_WIDTH + XA_WIDTH, x, g, w_in, kv, (w_grp, scale),
        [_const_spec((N_POOL_GROUPS, POOL_GROUP_DIM, POOL_GROUP_DIM)), _const_spec((1, TOK_WIDTH))],
        [pltpu.VMEM((TM + POOL_HALO, TOK_WIDTH), F32)], "pool_mixer")


def _sgu_mixer(x, g, w_in, kv, norm_g, w_s, bs_rows):
    return _mixer_call(
        _sgu_kernel, 2 * TOK_WIDTH + XA_WIDTH, x, g, w_in, kv, (norm_g, w_s, bs_rows),
        [_const_spec((1, TOK_WIDTH)), _const_spec((N_SGU_GROUPS, SGU_CHUNK, SGU_CHUNK)),
         _const_spec((SGU_CHUNK, TOK_WIDTH))],
        [pltpu.VMEM((TM, TOK_WIDTH), F32)], "sgu_mixer")


def _lane_pick(vals, lane, k):
    return jnp.sum(jnp.where(lane == k, vals, 0.0), axis=-1, keepdims=True)


def _outproj_kernel(cat_ref, x_ref, wout_ref, gffn_ref, wr_ref, br_ref,
                    xnew_ref, h2p_ref, route_ref, cnt_ref, carry_ref):
    j = pl.program_id(0)

    @pl.when(j == 0)
    def _():
        carry_ref[...] = jnp.zeros((1, LANES), F32)

    xn = x_ref[...] + jnp.dot(cat_ref[...], wout_ref[...], preferred_element_type=F32)
    xnew_ref[...] = xn
    h2b = _rms(xn, gffn_ref[...]).astype(BF16)
    logits = jnp.dot(h2b, wr_ref[...], preferred_element_type=F32) + br_ref[...]

    hf = h2b.astype(F32)
    hi = pltpu.bitcast(hf[:, :PACKED], jnp.uint32)
    lo = pltpu.bitcast(hf[:, PACKED:], jnp.uint32)
    h2p_ref[...] = hi | (lo >> 16)

    lane = lax.broadcasted_iota(jnp.int32, (TM, LANES), 1).astype(F32)
    neg = -jnp.inf
    big = float(LANES)
    gl = jnp.where(lane >= GROUP_LANE0, jnp.where(lane < GROUP_LANE0 + N_GROUPS, logits, neg), neg)
    gmax = jnp.max(gl, axis=-1, keepdims=True)
    p_g = 1.0 / jnp.sum(jnp.exp(gl - gmax), axis=-1, keepdims=True)
    g_lane = jnp.min(jnp.where(gl == gmax, lane, big), axis=-1, keepdims=True)
    e_lo = (g_lane - GROUP_LANE0) * EXPERTS_PER_GROUP
    el = jnp.where(lane >= e_lo, jnp.where(lane < e_lo + EXPERTS_PER_GROUP, logits, neg), neg)
    m1 = jnp.max(el, axis=-1, keepdims=True)
    i1 = jnp.min(jnp.where(el == m1, lane, big), axis=-1, keepdims=True)
    el2 = jnp.where(lane == i1, neg, el)
    m2 = jnp.max(el2, axis=-1, keepdims=True)
    i2 = jnp.min(jnp.where(el2 == m2, lane, big), axis=-1, keepdims=True)
    e2 = jnp.exp(m2 - m1)
    den = 1.0 + e2
    gate1 = p_g * (1.0 / den)
    gate2 = p_g * (e2 / den)

    oh1 = jnp.where(lane == i1, 1.0, 0.0)
    oh2 = jnp.where(lane == i2, 1.0, 0.0)
    both = oh1 + oh2
    r_i = lax.broadcasted_iota(jnp.int32, (TM, TM), 0)
    c_i = lax.broadcasted_iota(jnp.int32, (TM, TM), 1)
    tri = jnp.where(c_i < r_i, 1.0, 0.0).astype(BF16)
    before = jnp.dot(tri, both.astype(BF16), preferred_element_type=F32) + carry_ref[...]
    pos1 = jnp.sum(oh1 * before, axis=-1, keepdims=True)
    pos2 = jnp.sum(oh2 * before, axis=-1, keepdims=True)
    carry = carry_ref[...] + jnp.sum(both, axis=0, keepdims=True)
    carry_ref[...] = carry
    cnt_ref[...] = jnp.broadcast_to(carry, (8, LANES))

    route = jnp.where(lane == 0, gate1, 0.0)
    route = jnp.where(lane == 1, gate2, route)
    route = jnp.where(lane == 2, i1.astype(F32), route)
    route = jnp.where(lane == 3, i2.astype(F32), route)
    route = jnp.where(lane == 4, pos1, route)
    route = jnp.where(lane == 5, pos2, route)
    route_ref[...] = route


def _outproj(cat, x, w_out, g_ffn, w_r, b_r):
    return pl.pallas_call(
        _outproj_kernel,
        grid=(N_TOK // TM,),
        in_specs=[
            pl.BlockSpec((TM, D_MODEL), lambda j: (j, 0)),
            pl.BlockSpec((TM, D_MODEL), lambda j: (j, 0)),
            _const_spec((D_MODEL, D_MODEL)),
            _const_spec((1, D_MODEL)),
            _const_spec((D_MODEL, LANES)),
            _const_spec((1, LANES)),
        ],
        out_specs=[
            pl.BlockSpec((TM, D_MODEL), lambda j: (j, 0)),
            pl.BlockSpec((TM, PACKED), lambda j: (j, 0)),
            pl.BlockSpec((TM, LANES), lambda j: (j, 0)),
            pl.BlockSpec((8, LANES), lambda j: (0, 0)),
        ],
        out_shape=[
            jax.ShapeDtypeStruct((N_TOK, D_MODEL), F32),
            jax.ShapeDtypeStruct((N_TOK, PACKED), jnp.uint32),
            jax.ShapeDtypeStruct((N_TOK, LANES), F32),
            jax.ShapeDtypeStruct((8, LANES), F32),
        ],
        scratch_shapes=[pltpu.VMEM((1, LANES), F32)],
        compiler_params=pltpu.CompilerParams(dimension_semantics=("arbitrary",), vmem_limit_bytes=VMEM_LIMIT),
        name="outproj_router",
    )(cat, x, w_out, g_ffn, w_r, b_r)


def _dest_kernel(route_ref, pstart_ref, dest_ref):
    route = route_ref[...]
    lane = lax.broadcasted_iota(jnp.int32, (TM, LANES), 1)
    pstart = pstart_ref[...]
    out = jnp.zeros((TM, LANES), F32)
    for k in range(TOP_K):
        eid = _lane_pick(route, lane, 2 + k).astype(jnp.int32)
        base = jnp.sum(jnp.where(lane == eid, pstart, 0.0), axis=-1, keepdims=True)
        out = jnp.where(lane == k, base + _lane_pick(route, lane, 4 + k), out)
    dest_ref[...] = out.astype(jnp.int32)


def _dest_slots(route, pstart_row):
    return pl.pallas_call(
        _dest_kernel,
        grid=(N_TOK // TM,),
        in_specs=[pl.BlockSpec((TM, LANES), lambda j: (j, 0)), _const_spec((1, LANES))],
        out_specs=pl.BlockSpec((TM, LANES), lambda j: (j, 0)),
        out_shape=jax.ShapeDtypeStruct((N_TOK, LANES), jnp.int32),
        compiler_params=pltpu.CompilerParams(dimension_semantics=("arbitrary",), vmem_limit_bytes=VMEM_LIMIT),
        name="dest_slots",
    )(route, pstart_row)


PAIRS_PER_STEP = TM * TOP_K


def _row_copy(src_hbm, src_row, dst_hbm, dst_row, sem):
    return pltpu.make_async_copy(src_hbm.at[pl.ds(src_row, 1)], dst_hbm.at[pl.ds(dst_row, 1)], sem)


def _dispatch_kernel(cnt_ref, pstart_ref, dest_hbm, h2p_hbm, xs_hbm, idx_ref, zero_ref, sem_idx, sem_row):
    j = pl.program_id(0)
    idx_copy = pltpu.make_async_copy(dest_hbm.at[j], idx_ref, sem_idx)
    idx_copy.start()

    @pl.when(j == 0)
    def _():
        zero_ref[...] = jnp.zeros((8, PACKED), jnp.uint32)

        def per_expert(e, c):
            n = cnt_ref[e]
            n_pad = (n + EXP_BLK - 1) // EXP_BLK * EXP_BLK - n
            base = pstart_ref[e] + n

            def start(r, c2):
                _row_copy(zero_ref, 0, xs_hbm, base + r, sem_row).start()
                return c2

            def wait(r, c2):
                _row_copy(zero_ref, 0, xs_hbm, base, sem_row).wait()
                return c2

            lax.fori_loop(0, n_pad, start, 0)
            lax.fori_loop(0, n_pad, wait, 0)
            return c

        lax.fori_loop(0, N_EXPERTS, per_expert, 0)

    idx_copy.wait()

    def start(t, c):
        tok = j * TM + t
        _row_copy(h2p_hbm, tok, xs_hbm, idx_ref[2 * t], sem_row).start()
        _row_copy(h2p_hbm, tok, xs_hbm, idx_ref[2 * t + 1], sem_row).start()
        return c

    def wait(t, c):
        _row_copy(h2p_hbm, 0, xs_hbm, 0, sem_row).wait()
        _row_copy(h2p_hbm, 0, xs_hbm, 0, sem_row).wait()
        return c

    lax.fori_loop(0, TM, start, 0)
    lax.fori_loop(0, TM, wait, 0)


def _dispatch(cnt, pstart, dest2d, h2p):
    return pl.pallas_call(
        _dispatch_kernel,
        grid_spec=pltpu.PrefetchScalarGridSpec(
            num_scalar_prefetch=2,
            grid=(N_TOK // TM,),
            in_specs=[pl.BlockSpec(memory_space=pl.ANY), pl.BlockSpec(memory_space=pl.ANY)],
            out_specs=pl.BlockSpec(memory_space=pl.ANY),
            scratch_shapes=[
                pltpu.SMEM((PAIRS_PER_STEP,), jnp.int32),
                pltpu.VMEM((8, PACKED), jnp.uint32),
                pltpu.SemaphoreType.DMA,
                pltpu.SemaphoreType.DMA,
            ],
        ),
        out_shape=jax.ShapeDtypeStruct((N_SORTED, PACKED), jnp.uint32),
        compiler_params=pltpu.CompilerParams(dimension_semantics=("arbitrary",), vmem_limit_bytes=VMEM_LIMIT,
                                             has_side_effects=True),
        name="dispatch_rows",
    )(cnt, pstart, dest2d, h2p)


def _expert_kernel(blk_e_ref, nused_ref, xs_ref, wgu_ref, wd_ref, o_ref):
    i = pl.program_id(0)

    @pl.when(i < nused_ref[0])
    def _():
        w = xs_ref[...]
        hi = pltpu.bitcast(w & jnp.uint32(0xFFFF0000), F32).astype(BF16)
        lo = pltpu.bitcast(w << 16, F32).astype(BF16)
        gu = (jnp.dot(hi, wgu_ref[0, :PACKED, :], preferred_element_type=F32)
              + jnp.dot(lo, wgu_ref[0, PACKED:, :], preferred_element_type=F32))
        g = gu[:, :D_EXPERT]
        u = gu[:, D_EXPERT:]
        a = (g * (1.0 / (1.0 + jnp.exp(-g))) * u).astype(BF16)
        o_ref[...] = jnp.dot(a, wd_ref[0], preferred_element_type=F32)

    @pl.when(i >= nused_ref[0])
    def _():
        o_ref[...] = jnp.zeros((EXP_BLK, D_MODEL), F32)


def _experts(blk_e, nused, xs, w_gu, w_down):
    def xs_map(i, blk_e_ref, nused_ref):
        return (jnp.minimum(i, nused_ref[0] - 1), 0)

    def w_map(i, blk_e_ref, nused_ref):
        return (blk_e_ref[i], 0, 0)

    return pl.pallas_call(
        _expert_kernel,
        grid_spec=pltpu.PrefetchScalarGridSpec(
            num_scalar_prefetch=2,
            grid=(N_EXP_BLOCKS,),
            in_specs=[
                pl.BlockSpec((EXP_BLK, PACKED), xs_map),
                pl.BlockSpec((1, D_MODEL, 2 * D_EXPERT), w_map),
                pl.BlockSpec((1, D_EXPERT, D_MODEL), w_map),
            ],
            out_specs=pl.BlockSpec((EXP_BLK, D_MODEL), lambda i, b, n: (i, 0)),
        ),
        out_shape=jax.ShapeDtypeStruct((N_SORTED, D_MODEL), F32),
        compiler_params=pltpu.CompilerParams(dimension_semantics=("arbitrary",), vmem_limit_bytes=VMEM_LIMIT),
        name="expert_ffn",
    )(blk_e, nused, xs, w_gu, w_down)


def _combine_kernel(final_norm, dest_hbm, os_hbm, x_ref, route_ref, gfin_ref, o_ref, idx_ref, rows_ref, sem_idx, sem_row):
    j = pl.program_id(0)
    idx_copy = pltpu.make_async_copy(dest_hbm.at[j], idx_ref, sem_idx)
    idx_copy.start()
    idx_copy.wait()

    def start(t, c):
        pltpu.make_async_copy(os_hbm.at[pl.ds(idx_ref[2 * t], 1)], rows_ref.at[0, pl.ds(t, 1)], sem_row).start()
        pltpu.make_async_copy(os_hbm.at[pl.ds(idx_ref[2 * t + 1], 1)], rows_ref.at[1, pl.ds(t, 1)], sem_row).start()
        return c

    def wait(t, c):
        pltpu.make_async_copy(os_hbm.at[pl.ds(0, 1)], rows_ref.at[0, pl.ds(0, 1)], sem_row).wait()
        pltpu.make_async_copy(os_hbm.at[pl.ds(0, 1)], rows_ref.at[1, pl.ds(0, 1)], sem_row).wait()
        return c

    lax.fori_loop(0, TM, start, 0)
    lax.fori_loop(0, TM, wait, 0)

    route = route_ref[...]
    g0 = route[:, 0:1]
    g1 = route[:, 1:2]
    xn = x_ref[...] + (rows_ref[0] * g0 + rows_ref[1] * g1)
    if final_norm:
        xn = _rms(xn, gfin_ref[...])
    o_ref[...] = xn


def _combine(dest2d, out_sorted, x, route, g_final, final_norm):
    return pl.pallas_call(
        functools.partial(_combine_kernel, final_norm),
        grid=(N_TOK // TM,),
        in_specs=[
            pl.BlockSpec(memory_space=pl.ANY),
            pl.BlockSpec(memory_space=pl.ANY),
            pl.BlockSpec((TM, D_MODEL), lambda j: (j, 0)),
            pl.BlockSpec((TM, LANES), lambda j: (j, 0)),
            _const_spec((1, D_MODEL)),
        ],
        out_specs=pl.BlockSpec((TM, D_MODEL), lambda j: (j, 0)),
        out_shape=jax.ShapeDtypeStruct((N_TOK, D_MODEL), F32),
        scratch_shapes=[
            pltpu.SMEM((PAIRS_PER_STEP,), jnp.int32),
            pltpu.VMEM((TOP_K, TM, D_MODEL), F32),
            pltpu.SemaphoreType.DMA,
            pltpu.SemaphoreType.DMA,
        ],
        compiler_params=pltpu.CompilerParams(dimension_semantics=("arbitrary",), vmem_limit_bytes=VMEM_LIMIT),
        name="combine_rows",
    )(dest2d, out_sorted, x, route, g_final)


def _moe(cat, x, w_out, g_ffn, w_group, b_group, w_expert, b_expert, w_gu, w_down, g_final, final_norm):
    w_r = jnp.zeros((D_MODEL, LANES), F32)
    w_r = w_r.at[:, :N_EXPERTS].set(w_expert).at[:, GROUP_LANE0:GROUP_LANE0 + N_GROUPS].set(w_group).astype(BF16)
    b_r = jnp.zeros((1, LANES), F32)
    b_r = b_r.at[0, :N_EXPERTS].set(b_expert).at[0, GROUP_LANE0:GROUP_LANE0 + N_GROUPS].set(b_group)

    x_new, h2p, route, counts = _outproj(cat, x, w_out.astype(BF16), g_ffn.reshape(1, D_MODEL), w_r, b_r)

    cnt = counts[0, :N_EXPERTS].astype(jnp.int32)
    padded = (cnt + EXP_BLK - 1) // EXP_BLK * EXP_BLK
    pends = jnp.cumsum(padded)
    pstart = pends - padded
    nused = (pends[-1] // EXP_BLK).astype(jnp.int32)
    blk = jnp.arange(N_EXP_BLOCKS, dtype=jnp.int32)
    blk_e = jnp.sum((blk[:, None] * EXP_BLK >= pends[None, :]).astype(jnp.int32), axis=1)
    blk_e = jnp.minimum(blk_e, N_EXPERTS - 1)
    last_e = jnp.sum(jnp.where(blk == nused - 1, blk_e, 0))
    blk_e = jnp.where(blk < nused, blk_e, last_e).astype(jnp.int32)
    pstart_row = jnp.zeros((1, LANES), F32).at[0, :N_EXPERTS].set(pstart.astype(F32))

    dest = _dest_slots(route, pstart_row)
    dest2d = dest[:, :TOP_K].reshape(N_TOK // TM, PAIRS_PER_STEP)

    xs = _dispatch(cnt, pstart.astype(jnp.int32), dest2d, h2p)
    out_sorted = _experts(blk_e, nused.reshape(1), xs, w_gu.astype(BF16), w_down.astype(BF16))
    return _combine(dest2d, out_sorted, x_new, route, g_final.reshape(1, D_MODEL), final_norm)


def kernel(x, mem, norm_mix_g, norm_mem_g, norm_ffn_g, pool_w_in, pool_w_grp, pool_scale, pool_w_out, sgu_w_in, sgu_norm_g, sgu_w_s, sgu_b_s, sgu_w_out, xa_w_kv, moe_w_group, moe_b_group, moe_w_expert, moe_b_expert, moe_w_gu, moe_w_down, final_norm_g):
    xt = x.reshape(N_TOK, D_MODEL)
    mem2d = mem.reshape(BATCH * N_MEM, D_MODEL)
    for i in range(DEPTH):
        jm = i // 2
        g_mix = norm_mix_g[i].reshape(1, D_MODEL)
        kv = _kv_proj(mem2d, norm_mem_g[i].reshape(1, D_MODEL), xa_w_kv[i].astype(BF16))
        if i % 2 == 0:
            cat = _pool_mixer(xt, g_mix, pool_w_in[jm].astype(BF16), kv, pool_w_grp[jm].astype(BF16),
                              pool_scale[jm].reshape(1, TOK_WIDTH))
            w_out = pool_w_out[jm]
        else:
            bs_rows = jnp.repeat(sgu_b_s[jm].T, SGU_GROUP_DIM, axis=1)
            cat = _sgu_mixer(xt, g_mix, sgu_w_in[jm].astype(BF16), kv, sgu_norm_g[jm].reshape(1, TOK_WIDTH),
                             sgu_w_s[jm], bs_rows)
            w_out = sgu_w_out[jm]
        xt = _moe(cat, xt, w_out, norm_ffn_g[i], moe_w_group[i], moe_b_group[i], moe_w_expert[i], moe_b_expert[i],
                  moe_w_gu[i], moe_w_down[i], final_norm_g, i == DEPTH - 1)
    return xt.reshape(BATCH, SEQ, D_MODEL)
```

```python
import functools

import jax
import jax.numpy as jnp
from jax import lax
from jax.experimental import pallas as pl
from jax.experimental.pallas import tpu as pltpu

D_MODEL = 2048
BATCH = 2
SEQ = 16384
DEPTH = 2
N_TOK = BATCH * SEQ
TOK_WIDTH = 1024
N_POOL_GROUPS = 4
POOL_GROUP_DIM = 256
POOL_WINDOWS = (2, 4, 8, 16)
POOL_HALO = 16
SGU_CHUNK = 128
N_SGU_GROUPS = 4
SGU_GROUP_DIM = 256
N_MEM = 256
N_XA_HEADS = 4
XA_WIDTH = 1024
XA_HEAD_DIM = 256
N_GROUPS = 4
EXPERTS_PER_GROUP = 8
N_EXPERTS = 32
TOP_K = 2
D_EXPERT = 1024
EPS = 1e-6

LANES = 128
SUBLANES = 8
TM = 512
N_STEPS = N_TOK // TM
BLOCKS_PER_SEQ = SEQ // TM
N_PAIRS = N_TOK * TOP_K
PAGE = 256
PAGE_SHIFT = 8
N_PAGES = N_PAIRS // PAGE + N_EXPERTS
N_SORTED = N_PAGES * PAGE
PACKED = D_MODEL // 2
OUT_TILES = D_MODEL // (SUBLANES * LANES)
GROUP_LANE0 = N_EXPERTS
VMEM_LIMIT = 56 * 1024 * 1024
DMA_UNROLL = 8

F32 = jnp.float32
BF16 = jnp.bfloat16
I32 = jnp.int32
U32 = jnp.uint32


def _rms(x, g):
    ms = jnp.mean(x * x, axis=-1, keepdims=True)
    return x * lax.rsqrt(ms + EPS) * g


def _gelu_tanh(x):
    c = 0.7978845608028654
    return x * (0.5 * (1.0 + jnp.tanh(c * (x + 0.044715 * (x * x * x)))))


def _const_spec(shape):
    nd = len(shape)
    return pl.BlockSpec(shape, lambda *_: (0,) * nd, pipeline_mode=pl.Buffered(1))


def _params(**kw):
    return pltpu.CompilerParams(dimension_semantics=("arbitrary",), vmem_limit_bytes=VMEM_LIMIT, **kw)


def _kv_kernel(mem_ref, g_ref, w_ref, o_ref):
    j = pl.program_id(0)
    mn = _rms(mem_ref[...], g_ref[...]).astype(BF16)
    r = jnp.dot(mn, w_ref[...], preferred_element_type=F32)
    scale = jnp.where(j < 2, XA_HEAD_DIM ** -0.5, 1.0).astype(F32)
    o_ref[...] = (r * scale).astype(BF16)


def _kv_proj(mem2d, g, w_kv_bf16):
    nb = 4
    wn = 2 * XA_WIDTH // nb
    return pl.pallas_call(
        _kv_kernel,
        grid=(nb,),
        in_specs=[
            pl.BlockSpec((BATCH * N_MEM, D_MODEL), lambda j: (0, 0)),
            pl.BlockSpec((1, D_MODEL), lambda j: (0, 0)),
            pl.BlockSpec((D_MODEL, wn), lambda j: (0, j)),
        ],
        out_specs=pl.BlockSpec((BATCH * N_MEM, wn), lambda j: (0, j)),
        out_shape=jax.ShapeDtypeStruct((BATCH * N_MEM, 2 * XA_WIDTH), BF16),
        compiler_params=_params(),
        name="kv_proj",
    )(mem2d, g, w_kv_bf16)


def _cross_attention(h, w_in_ref, q_off, kv_ref, cat_ref):
    for hd in range(N_XA_HEADS):
        lo = hd * XA_HEAD_DIM
        q = jnp.dot(h, w_in_ref[:, q_off + lo:q_off + lo + XA_HEAD_DIM], preferred_element_type=F32).astype(BF16)
        k = kv_ref[:, lo:lo + XA_HEAD_DIM]
        v = kv_ref[:, XA_WIDTH + lo:XA_WIDTH + lo + XA_HEAD_DIM]
        s = lax.dot_general(q, k, (((1,), (1,)), ((), ())), preferred_element_type=F32)
        m = jnp.max(s, axis=-1, keepdims=True)
        e = jnp.exp(s - m)
        p = (e / jnp.sum(e, axis=-1, keepdims=True)).astype(BF16)
        o = jnp.dot(p, v, preferred_element_type=F32)
        cat_ref[:, TOK_WIDTH + lo:TOK_WIDTH + lo + XA_HEAD_DIM] = o.astype(BF16)


def _pool_kernel(x_ref, g_ref, w_in_ref, kv_ref, wgrp_ref, scale_ref, cat_ref, ext_ref):
    j = pl.program_id(0)
    jb = j % BLOCKS_PER_SEQ
    h = _rms(x_ref[...], g_ref[...]).astype(BF16)

    @pl.when(jb == 0)
    def _():
        ext_ref[0:POOL_HALO, :] = jnp.zeros((POOL_HALO, TOK_WIDTH), F32)

    pos = jb * TM + lax.broadcasted_iota(I32, (TM, 1), 0)
    for g in range(N_POOL_GROUPS):
        c0 = g * POOL_GROUP_DIM
        c1 = c0 + POOL_GROUP_DIM
        p = jnp.dot(h, w_in_ref[:, c0:c1], preferred_element_type=F32)
        ext_ref[POOL_HALO:POOL_HALO + TM, c0:c1] = p
        a = ext_ref[:, c0:c1]
        for k in range(g + 1):
            a = a + pltpu.roll(a, 1 << k, 0)
        cnt = jnp.minimum(pos + 1, POOL_WINDOWS[g]).astype(F32)
        pooled = a[POOL_HALO:, :] / cnt - p
        mixed = jnp.dot(pooled.astype(BF16), wgrp_ref[g], preferred_element_type=F32) * scale_ref[:, c0:c1]
        cat_ref[:, c0:c1] = mixed.astype(BF16)
    ext_ref[0:POOL_HALO, :] = ext_ref[TM:TM + POOL_HALO, :]
    _cross_attention(h, w_in_ref, TOK_WIDTH, kv_ref, cat_ref)


def _sgu_kernel(x_ref, g_ref, w_in_ref, kv_ref, ng_ref, ws_ref, bs_ref, cat_ref, v_ref):
    h = _rms(x_ref[...], g_ref[...]).astype(BF16)
    ss = jnp.zeros((TM, 1), F32)
    for g in range(N_SGU_GROUPS):
        c0 = g * SGU_GROUP_DIM
        c1 = c0 + SGU_GROUP_DIM
        gv = _gelu_tanh(jnp.dot(h, w_in_ref[:, TOK_WIDTH + c0:TOK_WIDTH + c1], preferred_element_type=F32))
        v_ref[:, c0:c1] = gv
        ss = ss + jnp.sum(gv * gv, axis=-1, keepdims=True)
    r = lax.rsqrt(ss * (1.0 / TOK_WIDTH) + EPS)
    row = lax.broadcasted_iota(I32, (SGU_CHUNK, SGU_CHUNK), 0)
    col = lax.broadcasted_iota(I32, (SGU_CHUNK, SGU_CHUNK), 1)
    causal = col <= row
    for g in range(N_SGU_GROUPS):
        c0 = g * SGU_GROUP_DIM
        c1 = c0 + SGU_GROUP_DIM
        u = _gelu_tanh(jnp.dot(h, w_in_ref[:, c0:c1], preferred_element_type=F32))
        vn = (v_ref[:, c0:c1] * r * ng_ref[:, c0:c1]).astype(BF16)
        wc = jnp.where(causal, ws_ref[g], 0.0).astype(BF16)
        for n in range(TM // SGU_CHUNK):
            r0 = n * SGU_CHUNK
            r1 = r0 + SGU_CHUNK
            sp = jnp.dot(wc, vn[r0:r1, :], preferred_element_type=F32) + bs_ref[:, c0:c1]
            cat_ref[r0:r1, c0:c1] = (u[r0:r1, :] * sp).astype(BF16)
    _cross_attention(h, w_in_ref, 2 * TOK_WIDTH, kv_ref, cat_ref)


def _mixer_call(kernel, n_in, x, g, w_in, kv, extra, extra_specs, scratch, name):
    return pl.pallas_call(
        kernel,
        grid=(N_STEPS,),
        in_specs=[
            pl.BlockSpec((TM, D_MODEL), lambda j: (j, 0)),
            _const_spec((1, D_MODEL)),
            _const_spec((D_MODEL, n_in)),
            pl.BlockSpec((N_MEM, 2 * XA_WIDTH), lambda j: (j // BLOCKS_PER_SEQ, 0)),
        ] + extra_specs,
        out_specs=pl.BlockSpec((TM, D_MODEL), lambda j: (j, 0)),
        out_shape=jax.ShapeDtypeStruct((N_TOK, D_MODEL), BF16),
        scratch_shapes=scratch,
        compiler_params=_params(),
        name=name,
    )(x, g, w_in, kv, *extra)


def _pool_mixer(x, g, w_in, kv, w_grp, scale):
    return _mixer_call(
        _pool_kernel, TOK_WIDTH + XA_WIDTH, x, g, w_in, kv, (w_grp, scale),
        [_const_spec((N_POOL_GROUPS, POOL_GROUP_DIM, POOL_GROUP_DIM)), _const_spec((1, TOK_WIDTH))],
        [pltpu.VMEM((TM + POOL_HALO, TOK_WIDTH), F32)], "pool_mixer")


def _sgu_mixer(x, g, w_in, kv, norm_g, w_s, bs_rows):
    return _mixer_call(
        _sgu_kernel, 2 * TOK_WIDTH + XA_WIDTH, x, g, w_in, kv, (norm_g, w_s, bs_rows),
        [_const_spec((1, TOK_WIDTH)), _const_spec((N_SGU_GROUPS, SGU_CHUNK, SGU_CHUNK)),
         _const_spec((SGU_CHUNK, TOK_WIDTH))],
        [pltpu.VMEM((TM, TOK_WIDTH), F32)], "sgu_mixer")


def _pick(onehot, vec):
    return jnp.sum(onehot * vec, axis=-1, keepdims=True)


def _ceil_pages(n):
    return jnp.floor((n + (PAGE - 1)) * (1.0 / PAGE))


def _rows(ref, row, n_rows, tiles):
    unit = tiles * SUBLANES
    start = row * unit if isinstance(row, int) else pl.multiple_of(row * unit, unit)
    return ref.at[pl.ds(start, n_rows * unit)]


def _row_wait(src_ref, dst_ref, sem, tiles):
    pltpu.make_async_copy(_rows(src_ref, 0, 1, tiles), _rows(dst_ref, 0, 1, tiles), sem).wait()


def _to_tiles(ref, vals, tiles):
    n = vals.shape[0]
    chunks = tiles * SUBLANES
    for c in range(chunks):
        ref[pl.ds(c, n, stride=chunks), :] = vals[:, c * LANES:(c + 1) * LANES]


def _from_tiles(ref, first, n, tiles):
    chunks = tiles * SUBLANES
    return [ref[pl.ds(first * chunks + c, n, stride=chunks), :] for c in range(chunks)]


def _outproj_kernel(cat_ref, x_ref, wout_ref, gffn_ref, wr_ref, br_ref,
                    xnew_ref, route_ref, idx_ref, hist_ref, xs_hbm,
                    cnt_ref, cur_ref, npg_ref, act_ref, zero_ref, meta_ref,
                    idx_smem, meta_smem, sem_idx, sem_row, sem_zero):
    j = pl.program_id(0)
    slot = j % 2

    @pl.when(j == 0)
    def _():
        cnt_ref[...] = jnp.zeros((1, LANES), F32)
        cur_ref[...] = jnp.zeros((1, LANES), F32)
        npg_ref[...] = jnp.zeros((1, LANES), F32)
        zero_ref[...] = jnp.zeros((PAGE * SUBLANES, LANES), U32)

    def wait_rows(s):
        def body(t, c):
            for _ in range(TOP_K * DMA_UNROLL):
                _row_wait(act_ref.at[s], xs_hbm, sem_row.at[s], 1)
            return c
        lax.fori_loop(0, TM // DMA_UNROLL, body, 0)

    @pl.when(j >= 2)
    def _():
        wait_rows(slot)

    xn = x_ref[...] + jnp.dot(cat_ref[...], wout_ref[...], preferred_element_type=F32)
    xnew_ref[...] = xn
    h2b = _rms(xn, gffn_ref[...]).astype(BF16)
    logits = jnp.dot(h2b, wr_ref[...], preferred_element_type=F32) + br_ref[...]

    hf = h2b.astype(F32)
    hi = pltpu.bitcast(hf[:, :PACKED], U32)
    lo = pltpu.bitcast(hf[:, PACKED:], U32)
    _to_tiles(act_ref.at[slot], hi | (lo >> 16), 1)

    lane = lax.broadcasted_iota(I32, (TM, LANES), 1).astype(F32)
    neg = -jnp.inf
    big = float(LANES)
    gl = jnp.where(lane >= GROUP_LANE0, jnp.where(lane < GROUP_LANE0 + N_GROUPS, logits, neg), neg)
    gmax = jnp.max(gl, axis=-1, keepdims=True)
    p_g = 1.0 / jnp.sum(jnp.exp(gl - gmax), axis=-1, keepdims=True)
    g_lane = jnp.min(jnp.where(gl == gmax, lane, big), axis=-1, keepdims=True)
    e_lo = (g_lane - GROUP_LANE0) * EXPERTS_PER_GROUP
    el = jnp.where(lane >= e_lo, jnp.where(lane < e_lo + EXPERTS_PER_GROUP, logits, neg), neg)
    m1 = jnp.max(el, axis=-1, keepdims=True)
    i1 = jnp.min(jnp.where(el == m1, lane, big), axis=-1, keepdims=True)
    el2 = jnp.where(lane == i1, neg, el)
    m2 = jnp.max(el2, axis=-1, keepdims=True)
    i2 = jnp.min(jnp.where(el2 == m2, lane, big), axis=-1, keepdims=True)
    e2 = jnp.exp(m2 - m1)
    den = 1.0 + e2
    gate1 = p_g * (1.0 / den)
    gate2 = p_g * (e2 / den)

    oh = (jnp.where(lane == i1, 1.0, 0.0), jnp.where(lane == i2, 1.0, 0.0))
    both = oh[0] + oh[1]
    r_i = lax.broadcasted_iota(I32, (TM, TM), 0)
    c_i = lax.broadcasted_iota(I32, (TM, TM), 1)
    tri = jnp.where(c_i < r_i, 1.0, 0.0).astype(BF16)
    cnt0 = cnt_ref[...]
    before = jnp.dot(tri, both.astype(BF16), preferred_element_type=F32) + cnt0
    n_new = jnp.sum(both, axis=0, keepdims=True)
    cnt1 = cnt0 + n_new
    old_pages = _ceil_pages(cnt0)
    new_pages = _ceil_pages(cnt1) - old_pages
    a_i = lax.broadcasted_iota(I32, (LANES, LANES), 0)
    b_i = lax.broadcasted_iota(I32, (LANES, LANES), 1)
    upper = jnp.where(a_i < b_i, 1.0, 0.0).astype(BF16)
    prefix = jnp.dot(jnp.broadcast_to(new_pages, (SUBLANES, LANES)).astype(BF16), upper,
                     preferred_element_type=F32)[0:1, :]
    first_new = npg_ref[...] + prefix
    cur = cur_ref[...]

    dest = jnp.zeros((TM, LANES), F32)
    for k in range(TOP_K):
        pos = _pick(oh[k], before)
        kk = jnp.floor(pos * (1.0 / PAGE))
        off = pos - kk * PAGE
        old_k = _pick(oh[k], old_pages)
        page = jnp.where(kk < old_k, _pick(oh[k], cur), _pick(oh[k], first_new) + (kk - old_k))
        dest = jnp.where(lane == float(k), page * PAGE + off, dest)

    cur_ref[...] = jnp.where(new_pages > 0.0, first_new + new_pages - 1.0, cur)
    npg_ref[...] = npg_ref[...] + jnp.sum(new_pages, axis=-1, keepdims=True)
    cnt_ref[...] = cnt1
    hist_ref[0] = jnp.broadcast_to(cnt1, (SUBLANES, LANES))

    route = jnp.where(lane == 0.0, gate1, 0.0)
    route_ref[...] = jnp.where(lane == 1.0, gate2, route)

    idx_ref[0] = dest.T[0:SUBLANES, :].astype(I32)
    idx_copy = pltpu.make_async_copy(idx_ref.at[0], idx_smem, sem_idx)
    idx_copy.start()
    idx_copy.wait()

    def issue(i, c):
        t0 = i * DMA_UNROLL
        slots = [[idx_smem[k, t0 + u] for k in range(TOP_K)] for u in range(DMA_UNROLL)]
        for u in range(DMA_UNROLL):
            src = _rows(act_ref.at[slot], t0 + u, 1, 1)
            for k in range(TOP_K):
                pltpu.make_async_copy(src, _rows(xs_hbm, slots[u][k], 1, 1), sem_row.at[slot]).start()
        return c

    lax.fori_loop(0, TM // DMA_UNROLL, issue, 0)

    @pl.when(j == N_STEPS - 1)
    def _():
        wait_rows(1 - slot)
        wait_rows(slot)
        meta = jnp.where(lax.broadcasted_iota(I32, (SUBLANES, LANES), 0) == 0, cnt_ref[...],
                         jnp.where(lax.broadcasted_iota(I32, (SUBLANES, LANES), 0) == 1, cur_ref[...], npg_ref[...]))
        meta_ref[...] = meta.astype(I32)
        meta_copy = pltpu.make_async_copy(meta_ref, meta_smem, sem_idx)
        meta_copy.start()
        meta_copy.wait()

        def tails(do_start):
            def per_expert(e, c):
                n = meta_smem[0, e]
                rem = (PAGE - n % PAGE) % PAGE
                base = meta_smem[1, e] * PAGE + (PAGE - rem)
                for b in range(PAGE_SHIFT - 1, -1, -1):
                    size = 1 << b
                    start_row = base + ((rem >> (b + 1)) << (b + 1))

                    @pl.when((rem & size) != 0)
                    def _():
                        cp = pltpu.make_async_copy(_rows(zero_ref, 0, size, 1), _rows(xs_hbm, start_row, size, 1), sem_zero)
                        if do_start:
                            cp.start()
                        else:
                            cp.wait()
                return c

            lax.fori_loop(0, N_EXPERTS, per_expert, 0)

            def per_page(p, c):
                cp = pltpu.make_async_copy(zero_ref, _rows(xs_hbm, p * PAGE, PAGE, 1), sem_zero)
                if do_start:
                    cp.start()
                else:
                    cp.wait()
                return c

            lax.fori_loop(meta_smem[2, 0], N_PAGES, per_page, 0)

        tails(True)
        tails(False)


def _outproj(cat, x, w_out, g_ffn, w_r, b_r):
    return pl.pallas_call(
        _outproj_kernel,
        grid=(N_STEPS,),
        in_specs=[
            pl.BlockSpec((TM, D_MODEL), lambda j: (j, 0)),
            pl.BlockSpec((TM, D_MODEL), lambda j: (j, 0)),
            _const_spec((D_MODEL, D_MODEL)),
            _const_spec((1, D_MODEL)),
            _const_spec((D_MODEL, LANES)),
            _const_spec((1, LANES)),
        ],
        out_specs=[
            pl.BlockSpec((TM, D_MODEL), lambda j: (j, 0)),
            pl.BlockSpec((TM, LANES), lambda j: (j, 0)),
            pl.BlockSpec((1, SUBLANES, TM), lambda j: (j, 0, 0)),
            pl.BlockSpec((1, SUBLANES, LANES), lambda j: (j, 0, 0)),
            pl.BlockSpec(memory_space=pl.ANY),
        ],
        out_shape=[
            jax.ShapeDtypeStruct((N_TOK, D_MODEL), F32),
            jax.ShapeDtypeStruct((N_TOK, LANES), F32),
            jax.ShapeDtypeStruct((N_STEPS, SUBLANES, TM), I32),
            jax.ShapeDtypeStruct((N_STEPS, SUBLANES, LANES), F32),
            jax.ShapeDtypeStruct((N_SORTED * SUBLANES, LANES), U32),
        ],
        scratch_shapes=[
            pltpu.VMEM((1, LANES), F32),
            pltpu.VMEM((1, LANES), F32),
            pltpu.VMEM((1, LANES), F32),
            pltpu.VMEM((2, TM * SUBLANES, LANES), U32),
            pltpu.VMEM((PAGE * SUBLANES, LANES), U32),
            pltpu.VMEM((SUBLANES, LANES), I32),
            pltpu.SMEM((SUBLANES, TM), I32),
            pltpu.SMEM((SUBLANES, LANES), I32),
            pltpu.SemaphoreType.DMA,
            pltpu.SemaphoreType.DMA((2,)),
            pltpu.SemaphoreType.DMA,
        ],
        compiler_params=_params(),
        name="outproj_router",
    )(cat, x, w_out, g_ffn, w_r, b_r)


def _expert_kernel(order_ref, blk_e_ref, nused_ref, xs_ref, wgu_ref, wd_ref, o_ref):
    i = pl.program_id(0)

    @pl.when(i < nused_ref[0])
    def _():
        w = jnp.concatenate(_from_tiles(xs_ref, 0, PAGE, 1), axis=1)
        hi = pltpu.bitcast(w & jnp.uint32(0xFFFF0000), F32).astype(BF16)
        lo = pltpu.bitcast(w << 16, F32).astype(BF16)
        gu = (jnp.dot(hi, wgu_ref[0, 0, :PACKED, :], preferred_element_type=F32)
              + jnp.dot(lo, wgu_ref[0, 0, PACKED:, :], preferred_element_type=F32))
        g = gu[:, :D_EXPERT]
        u = gu[:, D_EXPERT:]
        a = (g * (1.0 / (1.0 + jnp.exp(-g))) * u).astype(BF16)
        _to_tiles(o_ref, jnp.dot(a, wd_ref[0, 0], preferred_element_type=F32), OUT_TILES)

    @pl.when(i >= nused_ref[0])
    def _():
        o_ref[...] = jnp.zeros((PAGE * OUT_TILES * SUBLANES, LANES), F32)


def _experts(layer, order, blk_e, nused, xs, w_gu, w_down):
    def page_map(i, order_ref, blk_e_ref, nused_ref):
        return (order_ref[i], 0)

    def w_map(i, order_ref, blk_e_ref, nused_ref):
        return (layer, blk_e_ref[i], 0, 0)

    return pl.pallas_call(
        _expert_kernel,
        grid_spec=pltpu.PrefetchScalarGridSpec(
            num_scalar_prefetch=3,
            grid=(N_PAGES,),
            in_specs=[
                pl.BlockSpec((PAGE * SUBLANES, LANES), page_map),
                pl.BlockSpec((1, 1, D_MODEL, 2 * D_EXPERT), w_map),
                pl.BlockSpec((1, 1, D_EXPERT, D_MODEL), w_map),
            ],
            out_specs=pl.BlockSpec((PAGE * OUT_TILES * SUBLANES, LANES), page_map),
        ),
        out_shape=jax.ShapeDtypeStruct((N_SORTED * OUT_TILES * SUBLANES, LANES), F32),
        compiler_params=_params(),
        name="expert_ffn",
    )(order, blk_e, nused, xs, w_gu, w_down)


def _combine_kernel(final_norm, idx_hbm, os_hbm, x_ref, route_ref, gfin_ref, o_ref, idx_smem, rows_ref, sem_idx, sem_row):
    j = pl.program_id(0)
    slot = j % 2

    def fetch(step, s):
        idx_copy = pltpu.make_async_copy(idx_hbm.at[step], idx_smem, sem_idx)
        idx_copy.start()
        idx_copy.wait()

        def issue(i, c):
            t0 = i * DMA_UNROLL
            slots = [[idx_smem[k, t0 + u] for k in range(TOP_K)] for u in range(DMA_UNROLL)]
            for u in range(DMA_UNROLL):
                for k in range(TOP_K):
                    pltpu.make_async_copy(_rows(os_hbm, slots[u][k], 1, OUT_TILES),
                                          _rows(rows_ref.at[s], k * TM + t0 + u, 1, OUT_TILES), sem_row.at[s]).start()
            return c

        lax.fori_loop(0, TM // DMA_UNROLL, issue, 0)

    @pl.when(j == 0)
    def _():
        fetch(0, 0)

    @pl.when(j + 1 < N_STEPS)
    def _():
        fetch(j + 1, 1 - slot)

    def wait(i, c):
        for _ in range(TOP_K * DMA_UNROLL):
            _row_wait(os_hbm, rows_ref.at[slot], sem_row.at[slot], OUT_TILES)
        return c

    lax.fori_loop(0, TM // DMA_UNROLL, wait, 0)

    route = route_ref[...]
    g0 = route[:, 0:1]
    g1 = route[:, 1:2]
    r0 = _from_tiles(rows_ref.at[slot], 0, TM, OUT_TILES)
    r1 = _from_tiles(rows_ref.at[slot], TM, TM, OUT_TILES)
    xn = jnp.concatenate([x_ref[:, c * LANES:(c + 1) * LANES] + (r0[c] * g0 + r1[c] * g1)
                          for c in range(OUT_TILES * SUBLANES)], axis=1)
    if final_norm:
        xn = _rms(xn, gfin_ref[...])
    o_ref[...] = xn


def _combine(idx, out_sorted, x, route, g_final, final_norm):
    return pl.pallas_call(
        functools.partial(_combine_kernel, final_norm),
        grid=(N_STEPS,),
        in_specs=[
            pl.BlockSpec(memory_space=pl.ANY),
            pl.BlockSpec(memory_space=pl.ANY),
            pl.BlockSpec((TM, D_MODEL), lambda j: (j, 0)),
            pl.BlockSpec((TM, LANES), lambda j: (j, 0)),
            _const_spec((1, D_MODEL)),
        ],
        out_specs=pl.BlockSpec((TM, D_MODEL), lambda j: (j, 0)),
        out_shape=jax.ShapeDtypeStruct((N_TOK, D_MODEL), F32),
        scratch_shapes=[
            pltpu.SMEM((SUBLANES, TM), I32),
            pltpu.VMEM((2, TOP_K * TM * OUT_TILES * SUBLANES, LANES), F32),
            pltpu.SemaphoreType.DMA,
            pltpu.SemaphoreType.DMA((2,)),
        ],
        compiler_params=_params(),
        name="combine_rows",
    )(idx, out_sorted, x, route, g_final)


def _page_schedule(hist):
    incl = hist[:, 0, :N_EXPERTS].astype(I32)
    excl = jnp.concatenate([jnp.zeros((1, N_EXPERTS), I32), incl[:-1]], axis=0)
    pages_after = (incl + PAGE - 1) // PAGE
    pages_before = (excl + PAGE - 1) // PAGE
    new_pages = (pages_after - pages_before).reshape(-1)
    seg_end = jnp.cumsum(new_pages)
    seg_start = seg_end - new_pages
    nused = seg_end[-1]
    page = jnp.arange(N_PAGES, dtype=I32)
    seg = jnp.sum((page[:, None] >= seg_end[None, :]).astype(I32), axis=1)
    seg = jnp.minimum(seg, N_STEPS * N_EXPERTS - 1)
    onehot_seg = (seg[:, None] == jnp.arange(N_STEPS * N_EXPERTS, dtype=I32)[None, :]).astype(I32)
    page_e = seg % N_EXPERTS
    k_in_e = page - jnp.sum(onehot_seg * seg_start[None, :], axis=1) + jnp.sum(onehot_seg * pages_before.reshape(-1)[None, :], axis=1)
    pages_per_e = pages_after[-1]
    e_start = jnp.cumsum(pages_per_e) - pages_per_e
    rank = jnp.sum((page_e[:, None] == jnp.arange(N_EXPERTS, dtype=I32)[None, :]).astype(I32) * e_start[None, :], axis=1) + k_in_e
    rank = jnp.where(page < nused, rank, page)
    order = jnp.sum((rank[None, :] == page[:, None]).astype(I32) * page[None, :], axis=1)
    e_of_order = jnp.sum((order[:, None] == page[None, :]).astype(I32) * page_e[None, :], axis=1)
    last_e = jnp.sum(jnp.where(page == nused - 1, e_of_order, 0))
    blk_e = jnp.where(page < nused, e_of_order, last_e)
    return order.astype(I32), blk_e.astype(I32), nused.astype(I32).reshape(1)


def _moe(layer, cat, x, w_out, g_ffn, w_group, b_group, w_expert, b_expert, w_gu, w_down, g_final, final_norm):
    w_r = jnp.zeros((D_MODEL, LANES), F32)
    w_r = w_r.at[:, :N_EXPERTS].set(w_expert).at[:, GROUP_LANE0:GROUP_LANE0 + N_GROUPS].set(w_group).astype(BF16)
    b_r = jnp.zeros((1, LANES), F32)
    b_r = b_r.at[0, :N_EXPERTS].set(b_expert).at[0, GROUP_LANE0:GROUP_LANE0 + N_GROUPS].set(b_group)

    x_new, route, idx, hist, xs = _outproj(cat, x, w_out.astype(BF16), g_ffn.reshape(1, D_MODEL), w_r, b_r)
    order, blk_e, nused = _page_schedule(hist)
    out_sorted = _experts(layer, order, blk_e, nused, xs, w_gu, w_down)
    return _combine(idx, out_sorted, x_new, route, g_final.reshape(1, D_MODEL), final_norm)


def kernel(x, mem, norm_mix_g, norm_mem_g, norm_ffn_g, pool_w_in, pool_w_grp, pool_scale, pool_w_out, sgu_w_in, sgu_norm_g, sgu_w_s, sgu_b_s, sgu_w_out, xa_w_kv, moe_w_group, moe_b_group, moe_w_expert, moe_b_expert, moe_w_gu, moe_w_down, final_norm_g):
    xt = x.reshape(N_TOK, D_MODEL)
    mem2d = mem.reshape(BATCH * N_MEM, D_MODEL)
    w_gu = moe_w_gu.astype(BF16)
    w_down = moe_w_down.astype(BF16)
    for i in range(DEPTH):
        jm = i // 2
        g_mix = norm_mix_g[i].reshape(1, D_MODEL)
        kv = _kv_proj(mem2d, norm_mem_g[i].reshape(1, D_MODEL), xa_w_kv[i].astype(BF16))
        if i % 2 == 0:
            cat = _pool_mixer(xt, g_mix, pool_w_in[jm].astype(BF16), kv, pool_w_grp[jm].astype(BF16),
                              pool_scale[jm].reshape(1, TOK_WIDTH))
            w_out = pool_w_out[jm]
        else:
            bs_rows = jnp.repeat(sgu_b_s[jm].T, SGU_GROUP_DIM, axis=1)
            cat = _sgu_mixer(xt, g_mix, sgu_w_in[jm].astype(BF16), kv, sgu_norm_g[jm].reshape(1, TOK_WIDTH),
                             sgu_w_s[jm], bs_rows)
            w_out = sgu_w_out[jm]
        xt = _moe(i, cat, xt, w_out, norm_ffn_g[i], moe_w_group[i], moe_b_group[i], moe_w_expert[i], moe_b_expert[i],
                  w_gu, w_down, final_norm_g, i == DEPTH - 1)
    return xt.reshape(BATCH, SEQ, D_MODEL)
```

```python
import functools

import jax
import jax.numpy as jnp
from jax import lax
from jax.experimental import pallas as pl
from jax.experimental.pallas import tpu as pltpu

D_MODEL = 2048
BATCH = 2
SEQ = 16384
DEPTH = 2
N_TOK = BATCH * SEQ
TOK_WIDTH = 1024
N_POOL_GROUPS = 4
POOL_GROUP_DIM = 256
POOL_WINDOWS = (2, 4, 8, 16)
POOL_HALO = 16
SGU_CHUNK = 128
N_SGU_GROUPS = 4
SGU_GROUP_DIM = 256
N_MEM = 256
N_XA_HEADS = 4
XA_WIDTH = 1024
XA_HEAD_DIM = 256
N_GROUPS = 4
EXPERTS_PER_GROUP = 8
N_EXPERTS = 32
TOP_K = 2
D_EXPERT = 1024
EPS = 1e-6

LANES = 128
SUBLANES = 8
TM = 512
N_STEPS = N_TOK // TM
BLOCKS_PER_SEQ = SEQ // TM
N_PAIRS = N_TOK * TOP_K
PAGE = 256
PAGE_SHIFT = 8
N_PAGES = N_PAIRS // PAGE + N_EXPERTS
N_SORTED = N_PAGES * PAGE
PACKED = D_MODEL // 2
OUT_TILES = D_MODEL // (SUBLANES * LANES)
GROUP_LANE0 = N_EXPERTS
VMEM_LIMIT = 56 * 1024 * 1024
DMA_UNROLL = 8

F32 = jnp.float32
BF16 = jnp.bfloat16
I32 = jnp.int32
U32 = jnp.uint32


def _rms(x, g):
    ms = jnp.mean(x * x, axis=-1, keepdims=True)
    return x * lax.rsqrt(ms + EPS) * g


def _gelu_tanh(x):
    c = 0.7978845608028654
    return x * (0.5 * (1.0 + jnp.tanh(c * (x + 0.044715 * (x * x * x)))))


def _const_spec(shape):
    nd = len(shape)
    return pl.BlockSpec(shape, lambda *_: (0,) * nd, pipeline_mode=pl.Buffered(1))


def _params(**kw):
    return pltpu.CompilerParams(dimension_semantics=("arbitrary",), vmem_limit_bytes=VMEM_LIMIT, **kw)


def _kv_kernel(mem_ref, g_ref, w_ref, o_ref):
    j = pl.program_id(0)
    mn = _rms(mem_ref[...], g_ref[...]).astype(BF16)
    r = jnp.dot(mn, w_ref[...], preferred_element_type=F32)
    scale = jnp.where(j < 2, XA_HEAD_DIM ** -0.5, 1.0).astype(F32)
    o_ref[...] = (r * scale).astype(BF16)


def _kv_proj(mem2d, g, w_kv_bf16):
    nb = 4
    wn = 2 * XA_WIDTH // nb
    return pl.pallas_call(
        _kv_kernel,
        grid=(nb,),
        in_specs=[
            pl.BlockSpec((BATCH * N_MEM, D_MODEL), lambda j: (0, 0)),
            pl.BlockSpec((1, D_MODEL), lambda j: (0, 0)),
            pl.BlockSpec((D_MODEL, wn), lambda j: (0, j)),
        ],
        out_specs=pl.BlockSpec((BATCH * N_MEM, wn), lambda j: (0, j)),
        out_shape=jax.ShapeDtypeStruct((BATCH * N_MEM, 2 * XA_WIDTH), BF16),
        compiler_params=_params(),
        name="kv_proj",
    )(mem2d, g, w_kv_bf16)


def _cross_attention(proj_ref, row0, q_off, kv_ref, cat_ref):
    for hd in range(N_XA_HEADS):
        lo = hd * XA_HEAD_DIM
        q = proj_ref[row0:row0 + TM, q_off + lo:q_off + lo + XA_HEAD_DIM].astype(BF16)
        k = kv_ref[:, lo:lo + XA_HEAD_DIM]
        v = kv_ref[:, XA_WIDTH + lo:XA_WIDTH + lo + XA_HEAD_DIM]
        s = lax.dot_general(q, k, (((1,), (1,)), ((), ())), preferred_element_type=F32)
        m = jnp.max(s, axis=-1, keepdims=True)
        e = jnp.exp(s - m)
        p = (e / jnp.sum(e, axis=-1, keepdims=True)).astype(BF16)
        o = jnp.dot(p, v, preferred_element_type=F32)
        cat_ref[:, TOK_WIDTH + lo:TOK_WIDTH + lo + XA_HEAD_DIM] = o.astype(BF16)


def _pool_kernel(x_ref, g_ref, w_in_ref, kv_ref, wgrp_ref, scale_ref, cat_ref, ext_ref):
    j = pl.program_id(0)
    jb = j % BLOCKS_PER_SEQ
    h = _rms(x_ref[...], g_ref[...]).astype(BF16)

    @pl.when(jb == 0)
    def _():
        ext_ref[0:POOL_HALO, :] = jnp.zeros((POOL_HALO, TOK_WIDTH + XA_WIDTH), F32)

    ext_ref[POOL_HALO:POOL_HALO + TM, :] = jnp.dot(h, w_in_ref[...], preferred_element_type=F32)
    pos = jb * TM + lax.broadcasted_iota(I32, (TM, 1), 0)
    for g in range(N_POOL_GROUPS):
        c0 = g * POOL_GROUP_DIM
        c1 = c0 + POOL_GROUP_DIM
        a = ext_ref[:, c0:c1]
        p = a[POOL_HALO:, :]
        for k in range(g + 1):
            a = a + pltpu.roll(a, 1 << k, 0)
        cnt = jnp.minimum(pos + 1, POOL_WINDOWS[g]).astype(F32)
        pooled = a[POOL_HALO:, :] / cnt - p
        mixed = jnp.dot(pooled.astype(BF16), wgrp_ref[g], preferred_element_type=F32) * scale_ref[:, c0:c1]
        cat_ref[:, c0:c1] = mixed.astype(BF16)
    ext_ref[0:POOL_HALO, :] = ext_ref[TM:TM + POOL_HALO, :]
    _cross_attention(ext_ref, POOL_HALO, TOK_WIDTH, kv_ref, cat_ref)


def _sgu_kernel(x_ref, g_ref, w_in_ref, kv_ref, ng_ref, ws_ref, bs_ref, cat_ref, proj_ref, v_ref):
    h = _rms(x_ref[...], g_ref[...]).astype(BF16)
    proj_ref[...] = jnp.dot(h, w_in_ref[...], preferred_element_type=F32)
    ss = jnp.zeros((TM, 1), F32)
    for g in range(N_SGU_GROUPS):
        c0 = g * SGU_GROUP_DIM
        c1 = c0 + SGU_GROUP_DIM
        gv = _gelu_tanh(proj_ref[:, TOK_WIDTH + c0:TOK_WIDTH + c1])
        v_ref[:, c0:c1] = gv
        ss = ss + jnp.sum(gv * gv, axis=-1, keepdims=True)
    r = lax.rsqrt(ss * (1.0 / TOK_WIDTH) + EPS)
    row = lax.broadcasted_iota(I32, (SGU_CHUNK, SGU_CHUNK), 0)
    col = lax.broadcasted_iota(I32, (SGU_CHUNK, SGU_CHUNK), 1)
    causal = col <= row
    for g in range(N_SGU_GROUPS):
        c0 = g * SGU_GROUP_DIM
        c1 = c0 + SGU_GROUP_DIM
        u = _gelu_tanh(proj_ref[:, c0:c1])
        vn = (v_ref[:, c0:c1] * r * ng_ref[:, c0:c1]).astype(BF16)
        wc = jnp.where(causal, ws_ref[g], 0.0).astype(BF16)
        for n in range(TM // SGU_CHUNK):
            r0 = n * SGU_CHUNK
            r1 = r0 + SGU_CHUNK
            sp = jnp.dot(wc, vn[r0:r1, :], preferred_element_type=F32) + bs_ref[:, c0:c1]
            cat_ref[r0:r1, c0:c1] = (u[r0:r1, :] * sp).astype(BF16)
    _cross_attention(proj_ref, 0, 2 * TOK_WIDTH, kv_ref, cat_ref)


def _mixer_call(kernel, n_in, x, g, w_in, kv, extra, extra_specs, scratch, name):
    return pl.pallas_call(
        kernel,
        grid=(N_STEPS,),
        in_specs=[
            pl.BlockSpec((TM, D_MODEL), lambda j: (j, 0)),
            _const_spec((1, D_MODEL)),
            _const_spec((D_MODEL, n_in)),
            pl.BlockSpec((N_MEM, 2 * XA_WIDTH), lambda j: (j // BLOCKS_PER_SEQ, 0)),
        ] + extra_specs,
        out_specs=pl.BlockSpec((TM, D_MODEL), lambda j: (j, 0)),
        out_shape=jax.ShapeDtypeStruct((N_TOK, D_MODEL), BF16),
        scratch_shapes=scratch,
        compiler_params=_params(),
        name=name,
    )(x, g, w_in, kv, *extra)


def _pool_mixer(x, g, w_in, kv, w_grp, scale):
    return _mixer_call(
        _pool_kernel, TOK_WIDTH + XA_WIDTH, x, g, w_in, kv, (w_grp, scale),
        [_const_spec((N_POOL_GROUPS, POOL_GROUP_DIM, POOL_GROUP_DIM)), _const_spec((1, TOK_WIDTH))],
        [pltpu.VMEM((TM + POOL_HALO, TOK_WIDTH + XA_WIDTH), F32)], "pool_mixer")


def _sgu_mixer(x, g, w_in, kv, norm_g, w_s, bs_rows):
    return _mixer_call(
        _sgu_kernel, 2 * TOK_WIDTH + XA_WIDTH, x, g, w_in, kv, (norm_g, w_s, bs_rows),
        [_const_spec((1, TOK_WIDTH)), _const_spec((N_SGU_GROUPS, SGU_CHUNK, SGU_CHUNK)),
         _const_spec((SGU_CHUNK, TOK_WIDTH))],
        [pltpu.VMEM((TM, 2 * TOK_WIDTH + XA_WIDTH), F32), pltpu.VMEM((TM, TOK_WIDTH), F32)], "sgu_mixer")


def _pick(onehot, vec):
    return jnp.sum(onehot * vec, axis=-1, keepdims=True)


def _ceil_pages(n):
    return jnp.floor((n + (PAGE - 1)) * (1.0 / PAGE))


def _rows(ref, row, n_rows, tiles):
    unit = tiles * SUBLANES
    start = row * unit if isinstance(row, int) else pl.multiple_of(row * unit, unit)
    return ref.at[pl.ds(start, n_rows * unit)]


def _row_wait(src_ref, dst_ref, sem, tiles):
    pltpu.make_async_copy(_rows(src_ref, 0, 1, tiles), _rows(dst_ref, 0, 1, tiles), sem).wait()


def _to_tiles(ref, vals, tiles):
    n = vals.shape[0]
    chunks = tiles * SUBLANES
    for c in range(chunks):
        ref[pl.ds(c, n, stride=chunks), :] = vals[:, c * LANES:(c + 1) * LANES]


def _from_tiles(ref, first, n, tiles):
    chunks = tiles * SUBLANES
    return [ref[pl.ds(first * chunks + c, n, stride=chunks), :] for c in range(chunks)]


def _load_slots(tile_ref, smem_refs, sem):
    copies = [pltpu.make_async_copy(tile_ref.at[k], smem_refs[k], sem) for k in range(TOP_K)]
    for cp in copies:
        cp.start()
    for cp in copies:
        cp.wait()


def _outproj_kernel(cat_ref, x_ref, wout_ref, gffn_ref, wr_ref, br_ref,
                    xnew_ref, route_ref, idx_ref, hist_ref, xs_hbm,
                    cnt_ref, cur_ref, npg_ref, act_ref, zero_ref, meta_ref,
                    idx0_smem, idx1_smem, meta_smem, sem_idx, sem_row, sem_zero):
    idx_smem = (idx0_smem, idx1_smem)
    j = pl.program_id(0)
    slot = j % 2

    @pl.when(j == 0)
    def _():
        cnt_ref[...] = jnp.zeros((1, LANES), F32)
        cur_ref[...] = jnp.zeros((1, LANES), F32)
        npg_ref[...] = jnp.zeros((1, LANES), F32)
        zero_ref[...] = jnp.zeros((PAGE * SUBLANES, LANES), U32)

    def wait_rows(s):
        def body(t, c):
            for _ in range(TOP_K * DMA_UNROLL):
                _row_wait(act_ref.at[s], xs_hbm, sem_row.at[s], 1)
            return c
        lax.fori_loop(0, TM // DMA_UNROLL, body, 0)

    @pl.when(j >= 2)
    def _():
        wait_rows(slot)

    xn = x_ref[...] + jnp.dot(cat_ref[...], wout_ref[...], preferred_element_type=F32)
    xnew_ref[...] = xn
    h2b = _rms(xn, gffn_ref[...]).astype(BF16)
    logits = jnp.dot(h2b, wr_ref[...], preferred_element_type=F32) + br_ref[...]

    hf = h2b.astype(F32)
    hi = pltpu.bitcast(hf[:, :PACKED], U32)
    lo = pltpu.bitcast(hf[:, PACKED:], U32)
    _to_tiles(act_ref.at[slot], hi | (lo >> 16), 1)

    lane = lax.broadcasted_iota(I32, (TM, LANES), 1).astype(F32)
    neg = -jnp.inf
    big = float(LANES)
    gl = jnp.where(lane >= GROUP_LANE0, jnp.where(lane < GROUP_LANE0 + N_GROUPS, logits, neg), neg)
    gmax = jnp.max(gl, axis=-1, keepdims=True)
    p_g = 1.0 / jnp.sum(jnp.exp(gl - gmax), axis=-1, keepdims=True)
    g_lane = jnp.min(jnp.where(gl == gmax, lane, big), axis=-1, keepdims=True)
    e_lo = (g_lane - GROUP_LANE0) * EXPERTS_PER_GROUP
    el = jnp.where(lane >= e_lo, jnp.where(lane < e_lo + EXPERTS_PER_GROUP, logits, neg), neg)
    m1 = jnp.max(el, axis=-1, keepdims=True)
    i1 = jnp.min(jnp.where(el == m1, lane, big), axis=-1, keepdims=True)
    el2 = jnp.where(lane == i1, neg, el)
    m2 = jnp.max(el2, axis=-1, keepdims=True)
    i2 = jnp.min(jnp.where(el2 == m2, lane, big), axis=-1, keepdims=True)
    e2 = jnp.exp(m2 - m1)
    den = 1.0 + e2
    gate1 = p_g * (1.0 / den)
    gate2 = p_g * (e2 / den)

    oh = (jnp.where(lane == i1, 1.0, 0.0), jnp.where(lane == i2, 1.0, 0.0))
    both = oh[0] + oh[1]
    r_i = lax.broadcasted_iota(I32, (TM, TM), 0)
    c_i = lax.broadcasted_iota(I32, (TM, TM), 1)
    tri = jnp.where(c_i < r_i, 1.0, 0.0).astype(BF16)
    cnt0 = cnt_ref[...]
    before = jnp.dot(tri, both.astype(BF16), preferred_element_type=F32) + cnt0
    n_new = jnp.sum(both, axis=0, keepdims=True)
    cnt1 = cnt0 + n_new
    old_pages = _ceil_pages(cnt0)
    new_pages = _ceil_pages(cnt1) - old_pages
    a_i = lax.broadcasted_iota(I32, (LANES, LANES), 0)
    b_i = lax.broadcasted_iota(I32, (LANES, LANES), 1)
    upper = jnp.where(a_i < b_i, 1.0, 0.0).astype(BF16)
    prefix = jnp.dot(jnp.broadcast_to(new_pages, (SUBLANES, LANES)).astype(BF16), upper,
                     preferred_element_type=F32)[0:1, :]
    first_new = npg_ref[...] + prefix
    cur = cur_ref[...]

    dest = jnp.zeros((TM, LANES), F32)
    for k in range(TOP_K):
        pos = _pick(oh[k], before)
        kk = jnp.floor(pos * (1.0 / PAGE))
        off = pos - kk * PAGE
        old_k = _pick(oh[k], old_pages)
        page = jnp.where(kk < old_k, _pick(oh[k], cur), _pick(oh[k], first_new) + (kk - old_k))
        dest = jnp.where(lane == float(k), page * PAGE + off, dest)

    cur_ref[...] = jnp.where(new_pages > 0.0, first_new + new_pages - 1.0, cur)
    npg_ref[...] = npg_ref[...] + jnp.sum(new_pages, axis=-1, keepdims=True)
    cnt_ref[...] = cnt1
    hist_ref[0] = jnp.broadcast_to(cnt1, (SUBLANES, LANES))

    route = jnp.where(lane == 0.0, gate1, 0.0)
    route_ref[...] = jnp.where(lane == 1.0, gate2, route)

    idx_ref[0] = dest.T[0:SUBLANES, :].astype(I32)
    _load_slots(idx_ref.at[0], idx_smem, sem_idx)

    def issue(i, c):
        t0 = i * DMA_UNROLL
        slots = [[idx_smem[k][t0 + u] for k in range(TOP_K)] for u in range(DMA_UNROLL)]
        for u in range(DMA_UNROLL):
            src = _rows(act_ref.at[slot], t0 + u, 1, 1)
            for k in range(TOP_K):
                pltpu.make_async_copy(src, _rows(xs_hbm, slots[u][k], 1, 1), sem_row.at[slot]).start()
        return c

    lax.fori_loop(0, TM // DMA_UNROLL, issue, 0)

    @pl.when(j == N_STEPS - 1)
    def _():
        wait_rows(1 - slot)
        wait_rows(slot)
        meta = jnp.where(lax.broadcasted_iota(I32, (SUBLANES, LANES), 0) == 0, cnt_ref[...],
                         jnp.where(lax.broadcasted_iota(I32, (SUBLANES, LANES), 0) == 1, cur_ref[...], npg_ref[...]))
        meta_ref[...] = meta.astype(I32)
        meta_copy = pltpu.make_async_copy(meta_ref, meta_smem, sem_idx)
        meta_copy.start()
        meta_copy.wait()

        def tails(do_start):
            def per_expert(e, c):
                n = meta_smem[0, e]
                rem = (PAGE - n % PAGE) % PAGE
                base = meta_smem[1, e] * PAGE + (PAGE - rem)
                for b in range(PAGE_SHIFT - 1, -1, -1):
                    size = 1 << b
                    start_row = base + ((rem >> (b + 1)) << (b + 1))

                    @pl.when((rem & size) != 0)
                    def _():
                        cp = pltpu.make_async_copy(_rows(zero_ref, 0, size, 1), _rows(xs_hbm, start_row, size, 1), sem_zero)
                        if do_start:
                            cp.start()
                        else:
                            cp.wait()
                return c

            lax.fori_loop(0, N_EXPERTS, per_expert, 0)

            def per_page(p, c):
                cp = pltpu.make_async_copy(zero_ref, _rows(xs_hbm, p * PAGE, PAGE, 1), sem_zero)
                if do_start:
                    cp.start()
                else:
                    cp.wait()
                return c

            lax.fori_loop(meta_smem[2, 0], N_PAGES, per_page, 0)

        tails(True)
        tails(False)


def _outproj(cat, x, w_out, g_ffn, w_r, b_r):
    return pl.pallas_call(
        _outproj_kernel,
        grid=(N_STEPS,),
        in_specs=[
            pl.BlockSpec((TM, D_MODEL), lambda j: (j, 0)),
            pl.BlockSpec((TM, D_MODEL), lambda j: (j, 0)),
            _const_spec((D_MODEL, D_MODEL)),
            _const_spec((1, D_MODEL)),
            _const_spec((D_MODEL, LANES)),
            _const_spec((1, LANES)),
        ],
        out_specs=[
            pl.BlockSpec((TM, D_MODEL), lambda j: (j, 0)),
            pl.BlockSpec((TM, LANES), lambda j: (j, 0)),
            pl.BlockSpec((1, SUBLANES, TM), lambda j: (j, 0, 0)),
            pl.BlockSpec((1, SUBLANES, LANES), lambda j: (j, 0, 0)),
            pl.BlockSpec(memory_space=pl.ANY),
        ],
        out_shape=[
            jax.ShapeDtypeStruct((N_TOK, D_MODEL), F32),
            jax.ShapeDtypeStruct((N_TOK, LANES), F32),
            jax.ShapeDtypeStruct((N_STEPS, SUBLANES, TM), I32),
            jax.ShapeDtypeStruct((N_STEPS, SUBLANES, LANES), F32),
            jax.ShapeDtypeStruct((N_SORTED * SUBLANES, LANES), U32),
        ],
        scratch_shapes=[
            pltpu.VMEM((1, LANES), F32),
            pltpu.VMEM((1, LANES), F32),
            pltpu.VMEM((1, LANES), F32),
            pltpu.VMEM((2, TM * SUBLANES, LANES), U32),
            pltpu.VMEM((PAGE * SUBLANES, LANES), U32),
            pltpu.VMEM((SUBLANES, LANES), I32),
            pltpu.SMEM((TM,), I32),
            pltpu.SMEM((TM,), I32),
            pltpu.SMEM((SUBLANES, LANES), I32),
            pltpu.SemaphoreType.DMA,
            pltpu.SemaphoreType.DMA((2,)),
            pltpu.SemaphoreType.DMA,
        ],
        compiler_params=_params(),
        name="outproj_router",
    )(cat, x, w_out, g_ffn, w_r, b_r)


def _pack_bf16_pairs(v):
    vf = v.astype(BF16).astype(F32)
    return pltpu.bitcast(vf[:, :PACKED], U32) | (pltpu.bitcast(vf[:, PACKED:], U32) >> 16)


def _unpack_hi(w):
    return pltpu.bitcast(w & jnp.uint32(0xFFFF0000), F32)


def _unpack_lo(w):
    return pltpu.bitcast(w << 16, F32)


S_PAGE, S_MODE, S_LD_E, S_LD_C, S_LD_ON, S_SLOT, S_LD_SLOT = range(7)
MODE_IDLE, MODE_COMPUTE, MODE_ZERO = range(3)
LOAD_STEPS = 4
GU_CHUNK = D_MODEL // LOAD_STEPS
DN_CHUNK = D_EXPERT // LOAD_STEPS
N_SCHED = LOAD_STEPS + N_PAGES + (LOAD_STEPS - 1) * N_EXPERTS


def _expert_kernel(sched_ref, xs_ref, wgu_ref, wd_ref, o_ref, wgu_s, wd_s):
    s = pl.program_id(0)
    mode = sched_ref[S_MODE, s]

    @pl.when(sched_ref[S_LD_ON, s] == 1)
    def _():
        ls = sched_ref[S_LD_SLOT, s]
        c = sched_ref[S_LD_C, s]
        wgu_s[ls, pl.ds(pl.multiple_of(c * GU_CHUNK, GU_CHUNK), GU_CHUNK), :] = wgu_ref[0, 0].astype(BF16)
        wd_s[ls, pl.ds(pl.multiple_of(c * DN_CHUNK, DN_CHUNK), DN_CHUNK), :] = wd_ref[0, 0].astype(BF16)

    @pl.when(mode == MODE_COMPUTE)
    def _():
        sl = sched_ref[S_SLOT, s]
        w = jnp.concatenate(_from_tiles(xs_ref, 0, PAGE, 1), axis=1)
        hi = _unpack_hi(w).astype(BF16)
        lo = _unpack_lo(w).astype(BF16)
        gu = (jnp.dot(hi, wgu_s[sl, :PACKED, :], preferred_element_type=F32)
              + jnp.dot(lo, wgu_s[sl, PACKED:, :], preferred_element_type=F32))
        g = gu[:, :D_EXPERT]
        u = gu[:, D_EXPERT:]
        a = (g * (1.0 / (1.0 + jnp.exp(-g))) * u).astype(BF16)
        out = jnp.dot(a, wd_s[sl], preferred_element_type=F32)
        _to_tiles(o_ref, _pack_bf16_pairs(out), 1)

    @pl.when(mode == MODE_ZERO)
    def _():
        o_ref[...] = jnp.zeros((PAGE * SUBLANES, LANES), U32)


def _experts(layer, sched, xs, w_gu, w_down):
    def page_map(s, sched_ref):
        return (sched_ref[S_PAGE, s], 0)

    def w_map(s, sched_ref):
        return (layer, sched_ref[S_LD_E, s], sched_ref[S_LD_C, s], 0)

    return pl.pallas_call(
        _expert_kernel,
        grid_spec=pltpu.PrefetchScalarGridSpec(
            num_scalar_prefetch=1,
            grid=(N_SCHED,),
            in_specs=[
                pl.BlockSpec((PAGE * SUBLANES, LANES), page_map),
                pl.BlockSpec((1, 1, GU_CHUNK, 2 * D_EXPERT), w_map),
                pl.BlockSpec((1, 1, DN_CHUNK, D_MODEL), w_map),
            ],
            out_specs=pl.BlockSpec((PAGE * SUBLANES, LANES), page_map),
            scratch_shapes=[
                pltpu.VMEM((2, D_MODEL, 2 * D_EXPERT), BF16),
                pltpu.VMEM((2, D_EXPERT, D_MODEL), BF16),
            ],
        ),
        out_shape=jax.ShapeDtypeStruct((N_SORTED * SUBLANES, LANES), U32),
        compiler_params=_params(),
        name="expert_ffn",
    )(sched, xs, w_gu, w_down)


def _combine_kernel(final_norm, idx_hbm, os_hbm, x_ref, route_ref, gfin_ref, o_ref,
                    idx0_smem, idx1_smem, rows_ref, sem_idx, sem_row):
    idx_smem = (idx0_smem, idx1_smem)
    j = pl.program_id(0)
    slot = j % 2

    def fetch(step, s):
        _load_slots(idx_hbm.at[step], idx_smem, sem_idx)

        def issue(i, c):
            t0 = i * DMA_UNROLL
            slots = [[idx_smem[k][t0 + u] for k in range(TOP_K)] for u in range(DMA_UNROLL)]
            for u in range(DMA_UNROLL):
                for k in range(TOP_K):
                    pltpu.make_async_copy(_rows(os_hbm, slots[u][k], 1, 1),
                                          _rows(rows_ref.at[s], k * TM + t0 + u, 1, 1), sem_row.at[s]).start()
            return c

        lax.fori_loop(0, TM // DMA_UNROLL, issue, 0)

    @pl.when(j == 0)
    def _():
        fetch(0, 0)

    @pl.when(j + 1 < N_STEPS)
    def _():
        fetch(j + 1, 1 - slot)

    def wait(i, c):
        for _ in range(TOP_K * DMA_UNROLL):
            _row_wait(os_hbm, rows_ref.at[slot], sem_row.at[slot], 1)
        return c

    lax.fori_loop(0, TM // DMA_UNROLL, wait, 0)

    route = route_ref[...]
    g0 = route[:, 0:1]
    g1 = route[:, 1:2]
    r0 = _from_tiles(rows_ref.at[slot], 0, TM, 1)
    r1 = _from_tiles(rows_ref.at[slot], TM, TM, 1)
    y = ([_unpack_hi(r0[c]) * g0 + _unpack_hi(r1[c]) * g1 for c in range(SUBLANES)]
         + [_unpack_lo(r0[c]) * g0 + _unpack_lo(r1[c]) * g1 for c in range(SUBLANES)])
    xn = jnp.concatenate([x_ref[:, c * LANES:(c + 1) * LANES] + y[c] for c in range(2 * SUBLANES)], axis=1)
    if final_norm:
        xn = _rms(xn, gfin_ref[...])
    o_ref[...] = xn


def _combine(idx, out_sorted, x, route, g_final, final_norm):
    return pl.pallas_call(
        functools.partial(_combine_kernel, final_norm),
        grid=(N_STEPS,),
        in_specs=[
            pl.BlockSpec(memory_space=pl.ANY),
            pl.BlockSpec(memory_space=pl.ANY),
            pl.BlockSpec((TM, D_MODEL), lambda j: (j, 0)),
            pl.BlockSpec((TM, LANES), lambda j: (j, 0)),
            _const_spec((1, D_MODEL)),
        ],
        out_specs=pl.BlockSpec((TM, D_MODEL), lambda j: (j, 0)),
        out_shape=jax.ShapeDtypeStruct((N_TOK, D_MODEL), F32),
        scratch_shapes=[
            pltpu.SMEM((TM,), I32),
            pltpu.SMEM((TM,), I32),
            pltpu.VMEM((2, TOP_K * TM * SUBLANES, LANES), U32),
            pltpu.SemaphoreType.DMA,
            pltpu.SemaphoreType.DMA((2,)),
        ],
        compiler_params=_params(),
        name="combine_rows",
    )(idx, out_sorted, x, route, g_final)


def _page_schedule(hist):
    incl = hist[:, 0, :N_EXPERTS].astype(I32)
    excl = jnp.concatenate([jnp.zeros((1, N_EXPERTS), I32), incl[:-1]], axis=0)
    pages_after = (incl + PAGE - 1) // PAGE
    pages_before = (excl + PAGE - 1) // PAGE
    new_pages = (pages_after - pages_before).reshape(-1)
    seg_end = jnp.cumsum(new_pages)
    seg_start = seg_end - new_pages
    nused = seg_end[-1]
    page = jnp.arange(N_PAGES, dtype=I32)
    seg = jnp.sum((page[:, None] >= seg_end[None, :]).astype(I32), axis=1)
    seg = jnp.minimum(seg, N_STEPS * N_EXPERTS - 1)
    onehot_seg = (seg[:, None] == jnp.arange(N_STEPS * N_EXPERTS, dtype=I32)[None, :]).astype(I32)
    page_e = seg % N_EXPERTS
    k_in_e = page - jnp.sum(onehot_seg * seg_start[None, :], axis=1) + jnp.sum(onehot_seg * pages_before.reshape(-1)[None, :], axis=1)
    pages_per_e = pages_after[-1]
    e_start = jnp.cumsum(pages_per_e) - pages_per_e
    rank = jnp.sum((page_e[:, None] == jnp.arange(N_EXPERTS, dtype=I32)[None, :]).astype(I32) * e_start[None, :], axis=1) + k_in_e
    rank = jnp.where(page < nused, rank, page)
    order = jnp.sum((rank[None, :] == page[:, None]).astype(I32) * page[None, :], axis=1)
    return order.astype(I32), pages_per_e.astype(I32), nused.astype(I32)


def _lookup(index, table):
    n = table.shape[0]
    return jnp.sum((index[:, None] == jnp.arange(n, dtype=I32)[None, :]).astype(I32) * table[None, :], axis=1)


def _expert_schedule(order, pages_per_e, nused):
    e_ids = jnp.arange(N_EXPERTS, dtype=I32)
    used = pages_per_e > 0
    steps_e = jnp.where(used, jnp.maximum(pages_per_e, LOAD_STEPS), 0)
    ends = LOAD_STEPS + jnp.cumsum(steps_e)
    starts = ends - steps_e
    total = ends[-1]
    s = jnp.arange(N_SCHED, dtype=I32)
    e_s = jnp.minimum(jnp.sum((s[:, None] >= ends[None, :]).astype(I32), axis=1), N_EXPERTS - 1)
    lead = s < LOAD_STEPS
    active = jnp.logical_and(s >= LOAD_STEPS, s < total)
    t = s - _lookup(e_s, starts)
    n_s = _lookup(e_s, pages_per_e)
    compute = jnp.logical_and(active, t < n_s)
    first_page_of_e = jnp.cumsum(pages_per_e) - pages_per_e
    rank_s = _lookup(e_s, first_page_of_e) + jnp.minimum(t, n_s - 1)
    n_zero = N_PAGES - nused
    z = s - total
    zero = jnp.logical_and(z >= 0, z < n_zero)
    last_page = jnp.where(n_zero > 0, N_PAGES - 1, _lookup((nused - 1).reshape(1), order)[0])
    page = jnp.where(lead, order[0], jnp.where(active, _lookup(rank_s, order), jnp.where(zero, nused + z, last_page)))
    mode = jnp.where(compute, MODE_COMPUTE, jnp.where(zero, MODE_ZERO, MODE_IDLE))

    cand = jnp.where(used, e_ids, N_EXPERTS)
    first_e = jnp.min(cand)
    last_e = jnp.max(jnp.where(used, e_ids, -1))
    nxt = jnp.min(jnp.where(jnp.logical_and(e_ids[None, :] > e_ids[:, None], used[None, :]), e_ids[None, :], N_EXPERTS), axis=1)
    nxt_s = _lookup(e_s, nxt)
    has_next = nxt_s < N_EXPERTS
    ld_e = jnp.where(lead, first_e, jnp.where(active, jnp.where(has_next, nxt_s, e_s), last_e))
    ld_c = jnp.where(lead, s, jnp.where(jnp.logical_and(active, has_next), jnp.minimum(t, LOAD_STEPS - 1), LOAD_STEPS - 1))
    ld_on = jnp.logical_or(lead, jnp.logical_and(jnp.logical_and(active, has_next), t < LOAD_STEPS))
    slot = _lookup(e_s, jnp.cumsum(used.astype(I32)) - 1) % 2
    ld_slot = jnp.where(lead, 0, 1 - slot)
    rows = [page, mode, ld_e, ld_c, ld_on.astype(I32), slot, ld_slot, jnp.zeros_like(s)]
    return jnp.stack([r.astype(I32) for r in rows], axis=0)


def _moe(layer, cat, x, w_out, g_ffn, w_group, b_group, w_expert, b_expert, w_gu, w_down, g_final, final_norm):
    w_r = jnp.zeros((D_MODEL, LANES), F32)
    w_r = w_r.at[:, :N_EXPERTS].set(w_expert).at[:, GROUP_LANE0:GROUP_LANE0 + N_GROUPS].set(w_group).astype(BF16)
    b_r = jnp.zeros((1, LANES), F32)
    b_r = b_r.at[0, :N_EXPERTS].set(b_expert).at[0, GROUP_LANE0:GROUP_LANE0 + N_GROUPS].set(b_group)

    x_new, route, idx, hist, xs = _outproj(cat, x, w_out.astype(BF16), g_ffn.reshape(1, D_MODEL), w_r, b_r)
    sched = _expert_schedule(*_page_schedule(hist))
    out_sorted = _experts(layer, sched, xs, w_gu, w_down)
    return _combine(idx, out_sorted, x_new, route, g_final.reshape(1, D_MODEL), final_norm)


def kernel(x, mem, norm_mix_g, norm_mem_g, norm_ffn_g, pool_w_in, pool_w_grp, pool_scale, pool_w_out, sgu_w_in, sgu_norm_g, sgu_w_s, sgu_b_s, sgu_w_out, xa_w_kv, moe_w_group, moe_b_group, moe_w_expert, moe_b_expert, moe_w_gu, moe_w_down, final_norm_g):
    xt = x.reshape(N_TOK, D_MODEL)
    mem2d = mem.reshape(BATCH * N_MEM, D_MODEL)
    for i in range(DEPTH):
        jm = i // 2
        g_mix = norm_mix_g[i].reshape(1, D_MODEL)
        kv = _kv_proj(mem2d, norm_mem_g[i].reshape(1, D_MODEL), xa_w_kv[i].astype(BF16))
        if i % 2 == 0:
            cat = _pool_mixer(xt, g_mix, pool_w_in[jm].astype(BF16), kv, pool_w_grp[jm].astype(BF16),
                              pool_scale[jm].reshape(1, TOK_WIDTH))
            w_out = pool_w_out[jm]
        else:
            bs_rows = jnp.repeat(sgu_b_s[jm].T, SGU_GROUP_DIM, axis=1)
            cat = _sgu_mixer(xt, g_mix, sgu_w_in[jm].astype(BF16), kv, sgu_norm_g[jm].reshape(1, TOK_WIDTH),
                             sgu_w_s[jm], bs_rows)
            w_out = sgu_w_out[jm]
        xt = _moe(i, cat, xt, w_out, norm_ffn_g[i], moe_w_group[i], moe_b_group[i], moe_w_expert[i], moe_b_expert[i],
                  moe_w_gu, moe_w_down, final_norm_g, i == DEPTH - 1)
    return xt.reshape(BATCH, SEQ, D_MODEL)
```

```python
import functools

import jax
import jax.numpy as jnp
from jax import lax
from jax.experimental import pallas as pl
from jax.experimental.pallas import tpu as pltpu

D_MODEL = 2048
BATCH = 2
SEQ = 16384
DEPTH = 2
N_TOK = BATCH * SEQ
TOK_WIDTH = 1024
N_POOL_GROUPS = 4
POOL_GROUP_DIM = 256
POOL_WINDOWS = (2, 4, 8, 16)
POOL_HALO = 16
SGU_CHUNK = 128
N_SGU_GROUPS = 4
SGU_GROUP_DIM = 256
N_MEM = 256
N_XA_HEADS = 4
XA_WIDTH = 1024
XA_HEAD_DIM = 256
N_GROUPS = 4
EXPERTS_PER_GROUP = 8
N_EXPERTS = 32
TOP_K = 2
D_EXPERT = 1024
EPS = 1e-6

LANES = 128
SUBLANES = 8
TM = 512
N_STEPS = N_TOK // TM
N_SUB = 2
SUB = TM // N_SUB
BLOCKS_PER_SEQ = SEQ // TM
N_PAIRS = N_TOK * TOP_K
PAGE = 256
PAGE_SHIFT = 8
N_PAGES = N_PAIRS // PAGE + N_EXPERTS
N_SORTED = N_PAGES * PAGE
PACKED = D_MODEL // 2
OUT_TILES = D_MODEL // (SUBLANES * LANES)
GROUP_LANE0 = N_EXPERTS
VMEM_LIMIT = 56 * 1024 * 1024
DMA_UNROLL = 8

F32 = jnp.float32
BF16 = jnp.bfloat16
I32 = jnp.int32
U32 = jnp.uint32


def _rms(x, g):
    ms = jnp.mean(x * x, axis=-1, keepdims=True)
    return x * lax.rsqrt(ms + EPS) * g


def _gelu_tanh(x):
    c = 0.7978845608028654
    return x * (0.5 * (1.0 + jnp.tanh(c * (x + 0.044715 * (x * x * x)))))


def _const_spec(shape):
    nd = len(shape)
    return pl.BlockSpec(shape, lambda *_: (0,) * nd, pipeline_mode=pl.Buffered(1))


def _params(**kw):
    return pltpu.CompilerParams(dimension_semantics=("arbitrary",), vmem_limit_bytes=VMEM_LIMIT, **kw)


def _kv_kernel(mem_ref, g_ref, w_ref, o_ref):
    j = pl.program_id(0)
    mn = _rms(mem_ref[...], g_ref[...]).astype(BF16)
    r = jnp.dot(mn, w_ref[...], preferred_element_type=F32)
    scale = jnp.where(j < 2, XA_HEAD_DIM ** -0.5, 1.0).astype(F32)
    o_ref[...] = (r * scale).astype(BF16)


def _kv_proj(mem2d, g, w_kv_bf16):
    nb = 4
    wn = 2 * XA_WIDTH // nb
    return pl.pallas_call(
        _kv_kernel,
        grid=(nb,),
        in_specs=[
            pl.BlockSpec((BATCH * N_MEM, D_MODEL), lambda j: (0, 0)),
            pl.BlockSpec((1, D_MODEL), lambda j: (0, 0)),
            pl.BlockSpec((D_MODEL, wn), lambda j: (0, j)),
        ],
        out_specs=pl.BlockSpec((BATCH * N_MEM, wn), lambda j: (0, j)),
        out_shape=jax.ShapeDtypeStruct((BATCH * N_MEM, 2 * XA_WIDTH), BF16),
        compiler_params=_params(),
        name="kv_proj",
    )(mem2d, g, w_kv_bf16)


def _attention_scores(proj_ref, row0, q_off, kv_ref):
    scores = []
    for hd in range(N_XA_HEADS):
        lo = hd * XA_HEAD_DIM
        q = proj_ref[row0:row0 + TM, q_off + lo:q_off + lo + XA_HEAD_DIM].astype(BF16)
        k = kv_ref[:, lo:lo + XA_HEAD_DIM]
        scores.append(lax.dot_general(q, k, (((1,), (1,)), ((), ())), preferred_element_type=F32))
    return scores


def _attention_values(scores, kv_ref, cat_ref):
    for hd in range(N_XA_HEADS):
        lo = hd * XA_HEAD_DIM
        s = scores[hd]
        v = kv_ref[:, XA_WIDTH + lo:XA_WIDTH + lo + XA_HEAD_DIM]
        m = jnp.max(s, axis=-1, keepdims=True)
        e = jnp.exp(s - m)
        p = (e / jnp.sum(e, axis=-1, keepdims=True)).astype(BF16)
        o = jnp.dot(p, v, preferred_element_type=F32)
        cat_ref[:, TOK_WIDTH + lo:TOK_WIDTH + lo + XA_HEAD_DIM] = o.astype(BF16)


def _pool_kernel(x_ref, g_ref, w_in_ref, kv_ref, wgrp_ref, scale_ref, cat_ref, ext_ref):
    j = pl.program_id(0)
    jb = j % BLOCKS_PER_SEQ
    h = _rms(x_ref[...], g_ref[...]).astype(BF16)

    @pl.when(jb == 0)
    def _():
        ext_ref[0:POOL_HALO, :] = jnp.zeros((POOL_HALO, TOK_WIDTH + XA_WIDTH), F32)

    ext_ref[POOL_HALO:POOL_HALO + TM, :] = jnp.dot(h, w_in_ref[...], preferred_element_type=F32)
    scores = _attention_scores(ext_ref, POOL_HALO, TOK_WIDTH, kv_ref)
    pos = jb * TM + lax.broadcasted_iota(I32, (TM, 1), 0)
    for g in range(N_POOL_GROUPS):
        c0 = g * POOL_GROUP_DIM
        c1 = c0 + POOL_GROUP_DIM
        a = ext_ref[:, c0:c1]
        p = a[POOL_HALO:, :]
        for k in range(g + 1):
            a = a + pltpu.roll(a, 1 << k, 0)
        cnt = jnp.minimum(pos + 1, POOL_WINDOWS[g]).astype(F32)
        pooled = a[POOL_HALO:, :] / cnt - p
        mixed = jnp.dot(pooled.astype(BF16), wgrp_ref[g], preferred_element_type=F32) * scale_ref[:, c0:c1]
        cat_ref[:, c0:c1] = mixed.astype(BF16)
    ext_ref[0:POOL_HALO, :] = ext_ref[TM:TM + POOL_HALO, :]
    _attention_values(scores, kv_ref, cat_ref)


def _sgu_kernel(x_ref, g_ref, w_in_ref, kv_ref, ng_ref, ws_ref, bs_ref, cat_ref, proj_ref, v_ref):
    h = _rms(x_ref[...], g_ref[...]).astype(BF16)
    proj_ref[...] = jnp.dot(h, w_in_ref[...], preferred_element_type=F32)
    scores = _attention_scores(proj_ref, 0, 2 * TOK_WIDTH, kv_ref)
    ss = jnp.zeros((TM, 1), F32)
    for g in range(N_SGU_GROUPS):
        c0 = g * SGU_GROUP_DIM
        c1 = c0 + SGU_GROUP_DIM
        gv = _gelu_tanh(proj_ref[:, TOK_WIDTH + c0:TOK_WIDTH + c1])
        v_ref[:, c0:c1] = gv
        ss = ss + jnp.sum(gv * gv, axis=-1, keepdims=True)
    r = lax.rsqrt(ss * (1.0 / TOK_WIDTH) + EPS)
    row = lax.broadcasted_iota(I32, (SGU_CHUNK, SGU_CHUNK), 0)
    col = lax.broadcasted_iota(I32, (SGU_CHUNK, SGU_CHUNK), 1)
    causal = col <= row
    for g in range(N_SGU_GROUPS):
        c0 = g * SGU_GROUP_DIM
        c1 = c0 + SGU_GROUP_DIM
        u = _gelu_tanh(proj_ref[:, c0:c1])
        vn = (v_ref[:, c0:c1] * r * ng_ref[:, c0:c1]).astype(BF16)
        wc = jnp.where(causal, ws_ref[g], 0.0).astype(BF16)
        for n in range(TM // SGU_CHUNK):
            r0 = n * SGU_CHUNK
            r1 = r0 + SGU_CHUNK
            sp = jnp.dot(wc, vn[r0:r1, :], preferred_element_type=F32) + bs_ref[:, c0:c1]
            cat_ref[r0:r1, c0:c1] = (u[r0:r1, :] * sp).astype(BF16)
    _attention_values(scores, kv_ref, cat_ref)


def _mixer_call(kernel, n_in, x, g, w_in, kv, extra, extra_specs, scratch, name):
    return pl.pallas_call(
        kernel,
        grid=(N_STEPS,),
        in_specs=[
            pl.BlockSpec((TM, D_MODEL), lambda j: (j, 0)),
            _const_spec((1, D_MODEL)),
            _const_spec((D_MODEL, n_in)),
            pl.BlockSpec((N_MEM, 2 * XA_WIDTH), lambda j: (j // BLOCKS_PER_SEQ, 0)),
        ] + extra_specs,
        out_specs=pl.BlockSpec((TM, D_MODEL), lambda j: (j, 0)),
        out_shape=jax.ShapeDtypeStruct((N_TOK, D_MODEL), BF16),
        scratch_shapes=scratch,
        compiler_params=_params(),
        name=name,
    )(x, g, w_in, kv, *extra)


def _pool_mixer(x, g, w_in, kv, w_grp, scale):
    return _mixer_call(
        _pool_kernel, TOK_WIDTH + XA_WIDTH, x, g, w_in, kv, (w_grp, scale),
        [_const_spec((N_POOL_GROUPS, POOL_GROUP_DIM, POOL_GROUP_DIM)), _const_spec((1, TOK_WIDTH))],
        [pltpu.VMEM((TM + POOL_HALO, TOK_WIDTH + XA_WIDTH), F32)], "pool_mixer")


def _sgu_mixer(x, g, w_in, kv, norm_g, w_s, bs_rows):
    return _mixer_call(
        _sgu_kernel, 2 * TOK_WIDTH + XA_WIDTH, x, g, w_in, kv, (norm_g, w_s, bs_rows),
        [_const_spec((1, TOK_WIDTH)), _const_spec((N_SGU_GROUPS, SGU_CHUNK, SGU_CHUNK)),
         _const_spec((SGU_CHUNK, TOK_WIDTH))],
        [pltpu.VMEM((TM, 2 * TOK_WIDTH + XA_WIDTH), F32), pltpu.VMEM((TM, TOK_WIDTH), F32)], "sgu_mixer")


def _pick(onehot, vec):
    return jnp.sum(onehot * vec, axis=-1, keepdims=True)


def _ceil_pages(n):
    return jnp.floor((n + (PAGE - 1)) * (1.0 / PAGE))


def _rows(ref, row, n_rows, tiles):
    unit = tiles * SUBLANES
    start = row * unit if isinstance(row, int) else pl.multiple_of(row * unit, unit)
    return ref.at[pl.ds(start, n_rows * unit)]


def _row_wait(src_ref, dst_ref, sem, tiles):
    pltpu.make_async_copy(_rows(src_ref, 0, 1, tiles), _rows(dst_ref, 0, 1, tiles), sem).wait()


def _to_tiles(ref, vals, tiles, first=0):
    n = vals.shape[0]
    chunks = tiles * SUBLANES
    for c in range(chunks):
        ref[pl.ds(first * chunks + c, n, stride=chunks), :] = vals[:, c * LANES:(c + 1) * LANES]


def _from_tiles(ref, first, n, tiles):
    chunks = tiles * SUBLANES
    return [ref[pl.ds(first * chunks + c, n, stride=chunks), :] for c in range(chunks)]


def _load_slots(tile_ref, smem_refs, sem):
    copies = [pltpu.make_async_copy(tile_ref.at[k], smem_refs[k], sem) for k in range(TOP_K)]
    for cp in copies:
        cp.start()
    for cp in copies:
        cp.wait()


def _outproj_kernel(cat_ref, x_ref, wout_ref, gffn_ref, wr_ref, br_ref,
                    xnew_ref, route_ref, idx_ref, hist_ref, xs_hbm,
                    cnt_ref, cur_ref, npg_ref, act_ref, zero_ref, meta_ref,
                    idx0_smem, idx1_smem, meta_smem, sem_idx, sem_row, sem_zero):
    idx_smem = (idx0_smem, idx1_smem)
    j = pl.program_id(0)
    slot = j % 2

    @pl.when(j == 0)
    def _():
        cnt_ref[...] = jnp.zeros((1, LANES), F32)
        cur_ref[...] = jnp.zeros((1, LANES), F32)
        npg_ref[...] = jnp.zeros((1, LANES), F32)
        zero_ref[...] = jnp.zeros((PAGE * SUBLANES, LANES), U32)

    def wait_rows(s):
        def body(t, c):
            for _ in range(TOP_K * DMA_UNROLL):
                _row_wait(act_ref.at[s], xs_hbm, sem_row.at[s], 1)
            return c
        lax.fori_loop(0, TM // DMA_UNROLL, body, 0)

    @pl.when(j >= 2)
    def _():
        wait_rows(slot)

    lane = lax.broadcasted_iota(I32, (SUB, LANES), 1).astype(F32)
    r_i = lax.broadcasted_iota(I32, (SUB, SUB), 0)
    c_i = lax.broadcasted_iota(I32, (SUB, SUB), 1)
    tri = jnp.where(c_i < r_i, 1.0, 0.0).astype(BF16)
    a_i = lax.broadcasted_iota(I32, (LANES, LANES), 0)
    b_i = lax.broadcasted_iota(I32, (LANES, LANES), 1)
    upper = jnp.where(a_i < b_i, 1.0, 0.0).astype(BF16)
    for sb in range(N_SUB):
        r0 = sb * SUB
        xnew_ref[r0:r0 + SUB, :] = x_ref[r0:r0 + SUB, :] + jnp.dot(cat_ref[r0:r0 + SUB, :], wout_ref[...],
                                                                    preferred_element_type=F32)
    logits = []
    for sb in range(N_SUB):
        r0 = sb * SUB
        h2b = _rms(xnew_ref[r0:r0 + SUB, :], gffn_ref[...]).astype(BF16)
        logits.append(jnp.dot(h2b, wr_ref[...], preferred_element_type=F32) + br_ref[...])
        _to_tiles(act_ref.at[slot], _pack_bf16_pairs(h2b), 1, first=r0)
    for sb in range(N_SUB):
        _route_sub_block(sb, logits[sb], lane, tri, upper, route_ref, idx_ref, hist_ref, cnt_ref, cur_ref, npg_ref)
    _load_slots(idx_ref.at[0], idx_smem, sem_idx)
    _issue_and_finish(j, slot, idx_smem, act_ref, xs_hbm, zero_ref, meta_ref, meta_smem, cnt_ref, cur_ref, npg_ref,
                      sem_idx, sem_row, sem_zero, wait_rows)


def _route_sub_block(sb, logits, lane, tri, upper, route_ref, idx_ref, hist_ref, cnt_ref, cur_ref, npg_ref):
    r0 = sb * SUB
    r1 = r0 + SUB
    neg = -jnp.inf
    big = float(LANES)
    gl = jnp.where(lane >= GROUP_LANE0, jnp.where(lane < GROUP_LANE0 + N_GROUPS, logits, neg), neg)
    gmax = jnp.max(gl, axis=-1, keepdims=True)
    p_g = 1.0 / jnp.sum(jnp.exp(gl - gmax), axis=-1, keepdims=True)
    g_lane = jnp.min(jnp.where(gl == gmax, lane, big), axis=-1, keepdims=True)
    e_lo = (g_lane - GROUP_LANE0) * EXPERTS_PER_GROUP
    el = jnp.where(lane >= e_lo, jnp.where(lane < e_lo + EXPERTS_PER_GROUP, logits, neg), neg)
    m1 = jnp.max(el, axis=-1, keepdims=True)
    i1 = jnp.min(jnp.where(el == m1, lane, big), axis=-1, keepdims=True)
    el2 = jnp.where(lane == i1, neg, el)
    m2 = jnp.max(el2, axis=-1, keepdims=True)
    i2 = jnp.min(jnp.where(el2 == m2, lane, big), axis=-1, keepdims=True)
    e2 = jnp.exp(m2 - m1)
    den = 1.0 + e2
    gate1 = p_g * (1.0 / den)
    gate2 = p_g * (e2 / den)

    oh = (jnp.where(lane == i1, 1.0, 0.0), jnp.where(lane == i2, 1.0, 0.0))
    both = oh[0] + oh[1]
    cnt0 = cnt_ref[...]
    before = jnp.dot(tri, both.astype(BF16), preferred_element_type=F32) + cnt0
    n_new = jnp.sum(both, axis=0, keepdims=True)
    cnt1 = cnt0 + n_new
    old_pages = _ceil_pages(cnt0)
    new_pages = _ceil_pages(cnt1) - old_pages
    prefix = jnp.dot(jnp.broadcast_to(new_pages, (SUBLANES, LANES)).astype(BF16), upper,
                     preferred_element_type=F32)[0:1, :]
    first_new = npg_ref[...] + prefix
    cur = cur_ref[...]

    dest = jnp.zeros((SUB, LANES), F32)
    for k in range(TOP_K):
        pos = _pick(oh[k], before)
        kk = jnp.floor(pos * (1.0 / PAGE))
        off = pos - kk * PAGE
        old_k = _pick(oh[k], old_pages)
        page = jnp.where(kk < old_k, _pick(oh[k], cur), _pick(oh[k], first_new) + (kk - old_k))
        dest = jnp.where(lane == float(k), page * PAGE + off, dest)

    cur_ref[...] = jnp.where(new_pages > 0.0, first_new + new_pages - 1.0, cur)
    npg_ref[...] = npg_ref[...] + jnp.sum(new_pages, axis=-1, keepdims=True)
    cnt_ref[...] = cnt1
    hist_ref[sb] = jnp.broadcast_to(cnt1, (SUBLANES, LANES))

    route = jnp.where(lane == 0.0, gate1, 0.0)
    route_ref[r0:r1, :] = jnp.where(lane == 1.0, gate2, route)

    idx_ref[0, :, r0:r1] = dest.T[0:SUBLANES, :].astype(I32)


def _issue_and_finish(j, slot, idx_smem, act_ref, xs_hbm, zero_ref, meta_ref, meta_smem, cnt_ref, cur_ref, npg_ref,
                      sem_idx, sem_row, sem_zero, wait_rows):
    def issue(i, c):
        t0 = i * DMA_UNROLL
        slots = [[idx_smem[k][t0 + u] for k in range(TOP_K)] for u in range(DMA_UNROLL)]
        for u in range(DMA_UNROLL):
            src = _rows(act_ref.at[slot], t0 + u, 1, 1)
            for k in range(TOP_K):
                pltpu.make_async_copy(src, _rows(xs_hbm, slots[u][k], 1, 1), sem_row.at[slot]).start()
        return c

    lax.fori_loop(0, TM // DMA_UNROLL, issue, 0)

    @pl.when(j == N_STEPS - 1)
    def _():
        wait_rows(1 - slot)
        wait_rows(slot)
        meta = jnp.where(lax.broadcasted_iota(I32, (SUBLANES, LANES), 0) == 0, cnt_ref[...],
                         jnp.where(lax.broadcasted_iota(I32, (SUBLANES, LANES), 0) == 1, cur_ref[...], npg_ref[...]))
        meta_ref[...] = meta.astype(I32)
        meta_copy = pltpu.make_async_copy(meta_ref, meta_smem, sem_idx)
        meta_copy.start()
        meta_copy.wait()

        def tails(do_start):
            def per_expert(e, c):
                n = meta_smem[0, e]
                rem = (PAGE - n % PAGE) % PAGE
                base = meta_smem[1, e] * PAGE + (PAGE - rem)
                for b in range(PAGE_SHIFT - 1, -1, -1):
                    size = 1 << b
                    start_row = base + ((rem >> (b + 1)) << (b + 1))

                    @pl.when((rem & size) != 0)
                    def _():
                        cp = pltpu.make_async_copy(_rows(zero_ref, 0, size, 1), _rows(xs_hbm, start_row, size, 1), sem_zero)
                        if do_start:
                            cp.start()
                        else:
                            cp.wait()
                return c

            lax.fori_loop(0, N_EXPERTS, per_expert, 0)

            def per_page(p, c):
                cp = pltpu.make_async_copy(zero_ref, _rows(xs_hbm, p * PAGE, PAGE, 1), sem_zero)
                if do_start:
                    cp.start()
                else:
                    cp.wait()
                return c

            lax.fori_loop(meta_smem[2, 0], N_PAGES, per_page, 0)

        tails(True)
        tails(False)


def _outproj(cat, x, w_out, g_ffn, w_r, b_r):
    return pl.pallas_call(
        _outproj_kernel,
        grid=(N_STEPS,),
        in_specs=[
            pl.BlockSpec((TM, D_MODEL), lambda j: (j, 0)),
            pl.BlockSpec((TM, D_MODEL), lambda j: (j, 0)),
            _const_spec((D_MODEL, D_MODEL)),
            _const_spec((1, D_MODEL)),
            _const_spec((D_MODEL, LANES)),
            _const_spec((1, LANES)),
        ],
        out_specs=[
            pl.BlockSpec((TM, D_MODEL), lambda j: (j, 0)),
            pl.BlockSpec((TM, LANES), lambda j: (j, 0)),
            pl.BlockSpec((1, SUBLANES, TM), lambda j: (j, 0, 0)),
            pl.BlockSpec((N_SUB, SUBLANES, LANES), lambda j: (j, 0, 0)),
            pl.BlockSpec(memory_space=pl.ANY),
        ],
        out_shape=[
            jax.ShapeDtypeStruct((N_TOK, D_MODEL), F32),
            jax.ShapeDtypeStruct((N_TOK, LANES), F32),
            jax.ShapeDtypeStruct((N_STEPS, SUBLANES, TM), I32),
            jax.ShapeDtypeStruct((N_STEPS * N_SUB, SUBLANES, LANES), F32),
            jax.ShapeDtypeStruct((N_SORTED * SUBLANES, LANES), U32),
        ],
        scratch_shapes=[
            pltpu.VMEM((1, LANES), F32),
            pltpu.VMEM((1, LANES), F32),
            pltpu.VMEM((1, LANES), F32),
            pltpu.VMEM((2, TM * SUBLANES, LANES), U32),
            pltpu.VMEM((PAGE * SUBLANES, LANES), U32),
            pltpu.VMEM((SUBLANES, LANES), I32),
            pltpu.SMEM((TM,), I32),
            pltpu.SMEM((TM,), I32),
            pltpu.SMEM((SUBLANES, LANES), I32),
            pltpu.SemaphoreType.DMA,
            pltpu.SemaphoreType.DMA((2,)),
            pltpu.SemaphoreType.DMA,
        ],
        compiler_params=_params(),
        name="outproj_router",
    )(cat, x, w_out, g_ffn, w_r, b_r)


def _pack_bf16_pairs(v):
    vf = v.astype(BF16).astype(F32)
    return pltpu.bitcast(vf[:, :PACKED], U32) | (pltpu.bitcast(vf[:, PACKED:], U32) >> 16)


def _unpack_hi(w):
    return pltpu.bitcast(w & jnp.uint32(0xFFFF0000), F32)


def _unpack_lo(w):
    return pltpu.bitcast(w << 16, F32)


S_PAGE, S_MODE, S_LD_E, S_LD_C, S_LD_ON, S_SLOT, S_LD_SLOT = range(7)
MODE_IDLE, MODE_COMPUTE, MODE_ZERO = range(3)
LOAD_STEPS = 4
GU_CHUNK = D_MODEL // LOAD_STEPS
DN_CHUNK = D_EXPERT // LOAD_STEPS
N_SCHED = LOAD_STEPS + N_PAGES + (LOAD_STEPS - 1) * N_EXPERTS


def _expert_kernel(sched_ref, xs_ref, wgu_ref, wd_ref, o_ref, wgu_s, wd_s):
    s = pl.program_id(0)
    mode = sched_ref[S_MODE, s]

    @pl.when(sched_ref[S_LD_ON, s] == 1)
    def _():
        ls = sched_ref[S_LD_SLOT, s]
        c = sched_ref[S_LD_C, s]
        wgu_s[ls, pl.ds(pl.multiple_of(c * GU_CHUNK, GU_CHUNK), GU_CHUNK), :] = wgu_ref[0, 0].astype(BF16)
        wd_s[ls, pl.ds(pl.multiple_of(c * DN_CHUNK, DN_CHUNK), DN_CHUNK), :] = wd_ref[0, 0].astype(BF16)

    @pl.when(mode == MODE_COMPUTE)
    def _():
        sl = sched_ref[S_SLOT, s]
        w = jnp.concatenate(_from_tiles(xs_ref, 0, PAGE, 1), axis=1)
        hi = _unpack_hi(w).astype(BF16)
        lo = _unpack_lo(w).astype(BF16)
        gu = (jnp.dot(hi, wgu_s[sl, :PACKED, :], preferred_element_type=F32)
              + jnp.dot(lo, wgu_s[sl, PACKED:, :], preferred_element_type=F32))
        g = gu[:, :D_EXPERT]
        u = gu[:, D_EXPERT:]
        a = (g * (1.0 / (1.0 + jnp.exp(-g))) * u).astype(BF16)
        out = jnp.dot(a, wd_s[sl], preferred_element_type=F32)
        _to_tiles(o_ref, _pack_bf16_pairs(out), 1)

    @pl.when(mode == MODE_ZERO)
    def _():
        o_ref[...] = jnp.zeros((PAGE * SUBLANES, LANES), U32)


def _experts(layer, sched, xs, w_gu, w_down):
    def page_map(s, sched_ref):
        return (sched_ref[S_PAGE, s], 0)

    def w_map(s, sched_ref):
        return (layer, sched_ref[S_LD_E, s], sched_ref[S_LD_C, s], 0)

    return pl.pallas_call(
        _expert_kernel,
        grid_spec=pltpu.PrefetchScalarGridSpec(
            num_scalar_prefetch=1,
            grid=(N_SCHED,),
            in_specs=[
                pl.BlockSpec((PAGE * SUBLANES, LANES), page_map),
                pl.BlockSpec((1, 1, GU_CHUNK, 2 * D_EXPERT), w_map),
                pl.BlockSpec((1, 1, DN_CHUNK, D_MODEL), w_map),
            ],
            out_specs=pl.BlockSpec((PAGE * SUBLANES, LANES), page_map),
            scratch_shapes=[
                pltpu.VMEM((2, D_MODEL, 2 * D_EXPERT), BF16),
                pltpu.VMEM((2, D_EXPERT, D_MODEL), BF16),
            ],
        ),
        out_shape=jax.ShapeDtypeStruct((N_SORTED * SUBLANES, LANES), U32),
        compiler_params=_params(),
        name="expert_ffn",
    )(sched, xs, w_gu, w_down)


def _combine_kernel(final_norm, idx_hbm, os_hbm, x_ref, route_ref, gfin_ref, o_ref,
                    idx0_smem, idx1_smem, rows_ref, sem_idx, sem_row):
    idx_smem = (idx0_smem, idx1_smem)
    j = pl.program_id(0)
    slot = j % 2

    def fetch(step, s):
        _load_slots(idx_hbm.at[step], idx_smem, sem_idx)

        def issue(i, c):
            t0 = i * DMA_UNROLL
            slots = [[idx_smem[k][t0 + u] for k in range(TOP_K)] for u in range(DMA_UNROLL)]
            for u in range(DMA_UNROLL):
                for k in range(TOP_K):
                    pltpu.make_async_copy(_rows(os_hbm, slots[u][k], 1, 1),
                                          _rows(rows_ref.at[s], k * TM + t0 + u, 1, 1), sem_row.at[s]).start(priority=k)
            return c

        lax.fori_loop(0, TM // DMA_UNROLL, issue, 0)

    @pl.when(j == 0)
    def _():
        fetch(0, 0)

    @pl.when(j + 1 < N_STEPS)
    def _():
        fetch(j + 1, 1 - slot)

    def wait(i, c):
        for _ in range(TOP_K * DMA_UNROLL):
            _row_wait(os_hbm, rows_ref.at[slot], sem_row.at[slot], 1)
        return c

    lax.fori_loop(0, TM // DMA_UNROLL, wait, 0)

    route = route_ref[...]
    g0 = route[:, 0:1]
    g1 = route[:, 1:2]
    r0 = _from_tiles(rows_ref.at[slot], 0, TM, 1)
    r1 = _from_tiles(rows_ref.at[slot], TM, TM, 1)
    y = ([_unpack_hi(r0[c]) * g0 + _unpack_hi(r1[c]) * g1 for c in range(SUBLANES)]
         + [_unpack_lo(r0[c]) * g0 + _unpack_lo(r1[c]) * g1 for c in range(SUBLANES)])
    xn = jnp.concatenate([x_ref[:, c * LANES:(c + 1) * LANES] + y[c] for c in range(2 * SUBLANES)], axis=1)
    if final_norm:
        xn = _rms(xn, gfin_ref[...])
    o_ref[...] = xn


def _combine(idx, out_sorted, x, route, g_final, final_norm):
    return pl.pallas_call(
        functools.partial(_combine_kernel, final_norm),
        grid=(N_STEPS,),
        in_specs=[
            pl.BlockSpec(memory_space=pl.ANY),
            pl.BlockSpec(memory_space=pl.ANY),
            pl.BlockSpec((TM, D_MODEL), lambda j: (j, 0)),
            pl.BlockSpec((TM, LANES), lambda j: (j, 0)),
            _const_spec((1, D_MODEL)),
        ],
        out_specs=pl.BlockSpec((TM, D_MODEL), lambda j: (j, 0)),
        out_shape=jax.ShapeDtypeStruct((N_TOK, D_MODEL), F32),
        scratch_shapes=[
            pltpu.SMEM((TM,), I32),
            pltpu.SMEM((TM,), I32),
            pltpu.VMEM((2, TOP_K * TM * SUBLANES, LANES), U32),
            pltpu.SemaphoreType.DMA,
            pltpu.SemaphoreType.DMA((2,)),
        ],
        compiler_params=_params(),
        name="combine_rows",
    )(idx, out_sorted, x, route, g_final)


def _page_schedule(hist):
    incl = hist[:, 0, :N_EXPERTS].astype(I32)
    excl = jnp.concatenate([jnp.zeros((1, N_EXPERTS), I32), incl[:-1]], axis=0)
    pages_after = (incl + PAGE - 1) // PAGE
    pages_before = (excl + PAGE - 1) // PAGE
    new_pages = (pages_after - pages_before).reshape(-1)
    seg_end = jnp.cumsum(new_pages)
    seg_start = seg_end - new_pages
    nused = seg_end[-1]
    page = jnp.arange(N_PAGES, dtype=I32)
    seg = jnp.sum((page[:, None] >= seg_end[None, :]).astype(I32), axis=1)
    n_seg = new_pages.shape[0]
    seg = jnp.minimum(seg, n_seg - 1)
    onehot_seg = (seg[:, None] == jnp.arange(n_seg, dtype=I32)[None, :]).astype(I32)
    page_e = seg % N_EXPERTS
    k_in_e = page - jnp.sum(onehot_seg * seg_start[None, :], axis=1) + jnp.sum(onehot_seg * pages_before.reshape(-1)[None, :], axis=1)
    pages_per_e = pages_after[-1]
    e_start = jnp.cumsum(pages_per_e) - pages_per_e
    rank = jnp.sum((page_e[:, None] == jnp.arange(N_EXPERTS, dtype=I32)[None, :]).astype(I32) * e_start[None, :], axis=1) + k_in_e
    rank = jnp.where(page < nused, rank, page)
    order = jnp.sum((rank[None, :] == page[:, None]).astype(I32) * page[None, :], axis=1)
    return order.astype(I32), pages_per_e.astype(I32), nused.astype(I32)


def _lookup(index, table):
    n = table.shape[0]
    return jnp.sum((index[:, None] == jnp.arange(n, dtype=I32)[None, :]).astype(I32) * table[None, :], axis=1)


def _expert_schedule(order, pages_per_e, nused):
    e_ids = jnp.arange(N_EXPERTS, dtype=I32)
    used = pages_per_e > 0
    steps_e = jnp.where(used, jnp.maximum(pages_per_e, LOAD_STEPS), 0)
    ends = LOAD_STEPS + jnp.cumsum(steps_e)
    starts = ends - steps_e
    total = ends[-1]
    s = jnp.arange(N_SCHED, dtype=I32)
    e_s = jnp.minimum(jnp.sum((s[:, None] >= ends[None, :]).astype(I32), axis=1), N_EXPERTS - 1)
    lead = s < LOAD_STEPS
    active = jnp.logical_and(s >= LOAD_STEPS, s < total)
    t = s - _lookup(e_s, starts)
    n_s = _lookup(e_s, pages_per_e)
    compute = jnp.logical_and(active, t < n_s)
    first_page_of_e = jnp.cumsum(pages_per_e) - pages_per_e
    rank_s = _lookup(e_s, first_page_of_e) + jnp.minimum(t, n_s - 1)
    n_zero = N_PAGES - nused
    z = s - total
    zero = jnp.logical_and(z >= 0, z < n_zero)
    last_page = jnp.where(n_zero > 0, N_PAGES - 1, _lookup((nused - 1).reshape(1), order)[0])
    page = jnp.where(lead, order[0], jnp.where(active, _lookup(rank_s, order), jnp.where(zero, nused + z, last_page)))
    mode = jnp.where(compute, MODE_COMPUTE, jnp.where(zero, MODE_ZERO, MODE_IDLE))

    cand = jnp.where(used, e_ids, N_EXPERTS)
    first_e = jnp.min(cand)
    last_e = jnp.max(jnp.where(used, e_ids, -1))
    nxt = jnp.min(jnp.where(jnp.logical_and(e_ids[None, :] > e_ids[:, None], used[None, :]), e_ids[None, :], N_EXPERTS), axis=1)
    nxt_s = _lookup(e_s, nxt)
    has_next = nxt_s < N_EXPERTS
    ld_e = jnp.where(lead, first_e, jnp.where(active, jnp.where(has_next, nxt_s, e_s), last_e))
    ld_c = jnp.where(lead, s, jnp.where(jnp.logical_and(active, has_next), jnp.minimum(t, LOAD_STEPS - 1), LOAD_STEPS - 1))
    ld_on = jnp.logical_or(lead, jnp.logical_and(jnp.logical_and(active, has_next), t < LOAD_STEPS))
    slot = _lookup(e_s, jnp.cumsum(used.astype(I32)) - 1) % 2
    ld_slot = jnp.where(lead, 0, 1 - slot)
    rows = [page, mode, ld_e, ld_c, ld_on.astype(I32), slot, ld_slot, jnp.zeros_like(s)]
    return jnp.stack([r.astype(I32) for r in rows], axis=0)


def _moe(layer, cat, x, w_out, g_ffn, w_group, b_group, w_expert, b_expert, w_gu, w_down, g_final, final_norm):
    w_r = jnp.zeros((D_MODEL, LANES), F32)
    w_r = w_r.at[:, :N_EXPERTS].set(w_expert).at[:, GROUP_LANE0:GROUP_LANE0 + N_GROUPS].set(w_group).astype(BF16)
    b_r = jnp.zeros((1, LANES), F32)
    b_r = b_r.at[0, :N_EXPERTS].set(b_expert).at[0, GROUP_LANE0:GROUP_LANE0 + N_GROUPS].set(b_group)

    x_new, route, idx, hist, xs = _outproj(cat, x, w_out.astype(BF16), g_ffn.reshape(1, D_MODEL), w_r, b_r)
    sched = _expert_schedule(*_page_schedule(hist))
    out_sorted = _experts(layer, sched, xs, w_gu, w_down)
    return _combine(idx, out_sorted, x_new, route, g_final.reshape(1, D_MODEL), final_norm)


def kernel(x, mem, norm_mix_g, norm_mem_g, norm_ffn_g, pool_w_in, pool_w_grp, pool_scale, pool_w_out, sgu_w_in, sgu_norm_g, sgu_w_s, sgu_b_s, sgu_w_out, xa_w_kv, moe_w_group, moe_b_group, moe_w_expert, moe_b_expert, moe_w_gu, moe_w_down, final_norm_g):
    xt = x.reshape(N_TOK, D_MODEL)
    mem2d = mem.reshape(BATCH * N_MEM, D_MODEL)
    for i in range(DEPTH):
        jm = i // 2
        g_mix = norm_mix_g[i].reshape(1, D_MODEL)
        kv = _kv_proj(mem2d, norm_mem_g[i].reshape(1, D_MODEL), xa_w_kv[i].astype(BF16))
        if i % 2 == 0:
            cat = _pool_mixer(xt, g_mix, pool_w_in[jm].astype(BF16), kv, pool_w_grp[jm].astype(BF16),
                              pool_scale[jm].reshape(1, TOK_WIDTH))
            w_out = pool_w_out[jm]
        else:
            bs_rows = jnp.repeat(sgu_b_s[jm].T, SGU_GROUP_DIM, axis=1)
            cat = _sgu_mixer(xt, g_mix, sgu_w_in[jm].astype(BF16), kv, sgu_norm_g[jm].reshape(1, TOK_WIDTH),
                             sgu_w_s[jm], bs_rows)
            w_out = sgu_w_out[jm]
        xt = _moe(i, cat, xt, w_out, norm_ffn_g[i], moe_w_group[i], moe_b_group[i], moe_w_expert[i], moe_b_expert[i],
                  moe_w_gu, moe_w_down, final_norm_g, i == DEPTH - 1)
    return xt.reshape(BATCH, SEQ, D_MODEL)
```
